```python
import jax
import jax.numpy as jnp
from jax import lax
import numpy as np

D_MODEL = 1024
BATCH = 16
SEQ = 2048
DEPTH = 4
DEC_BATCH = 32
DEC_SEQ = 16
PAST_LEN = 4096

CHUNK = 64
N_A = DEPTH // 2
N_B = DEPTH - N_A
MIX_WIDTH = D_MODEL
MEM_WIDTH = D_MODEL // 4
MAIN_WIDTH = MIX_WIDTH - MEM_WIDTH
MEM_HEADS = 4
MEM_HEAD_DIM = MEM_WIDTH // MEM_HEADS
N_MEM = 256
A_HEADS = 4
A_HEAD_DIM = MAIN_WIDTH // A_HEADS
B_HEADS = 12
B_HEAD_DIM = MAIN_WIDTH // B_HEADS
A_IN_WIDTH = 4 * MAIN_WIDTH + 2 * A_HEADS + MEM_WIDTH
A_SPLITS = (MAIN_WIDTH, 2 * MAIN_WIDTH, 3 * MAIN_WIDTH, 4 * MAIN_WIDTH, 4 * MAIN_WIDTH + 2 * A_HEADS)
B_IN_WIDTH = MAIN_WIDTH + MEM_WIDTH
N_EXPERTS = 32
TOP_K = 4
D_FF = D_MODEL
SWIGLU_LIMIT = 7.0
SWIGLU_ALPHA = 1.702
Q_BLOCK = 128
DN_ALPHA = (2.0 * DEPTH) ** 0.25
DN_BETA = (8.0 * DEPTH) ** -0.25
LN_EPS = 1e-5

kernel_name = 'hybrid_mlstm_stickbreak_yoco_step'


def layer_norm(x, g, b):
    xf = x.astype(jnp.float32)
    mu = xf.mean(-1, keepdims=True)
    var = jnp.square(xf - mu).mean(-1, keepdims=True)
    return ((xf - mu) * lax.rsqrt(var + LN_EPS) * g + b).astype(x.dtype)


def mlstm_chunk(q, k, v, log_i, log_f, C, n, m):
    L = q.shape[2]
    b = jnp.cumsum(log_f, axis=-1)
    causal = jnp.tril(jnp.ones((L, L), dtype=bool))
    d = jnp.where(causal, b[..., :, None] - b[..., None, :] + log_i[..., None, :], -jnp.inf)
    inter = b + m[..., None]
    m_t = jnp.maximum(inter, d.max(-1))
    g = jnp.exp(inter - m_t)
    s = jnp.einsum('bhtd,bhsd->bhts', q, k) * jnp.exp(d - m_t[..., None])
    num = g[..., None] * jnp.einsum('bhtd,bhde->bhte', q, C) + jnp.einsum('bhts,bhse->bhte', s, v)
    den = g * jnp.einsum('bhtd,bhd->bht', q, n) + s.sum(-1)
    h = num / jnp.maximum(jnp.abs(den), jnp.exp(-m_t))[..., None]
    b_last = b[..., -1]
    dec = b_last[..., None] - b + log_i
    m_new = jnp.maximum(b_last + m, dec.max(-1))
    g_c = jnp.exp(b_last + m - m_new)
    w_s = jnp.exp(dec - m_new[..., None])[..., None]
    C_new = g_c[..., None, None] * C + jnp.einsum('bhsd,bhse->bhde', k * w_s, v)
    n_new = g_c[..., None] * n + jnp.sum(k * w_s, axis=2)
    return h, C_new, n_new, m_new


def mlstm_seq(q, k, v, log_i, log_f, C, n, m):
    B, H, T = log_i.shape
    chunk = min(CHUNK, T)
    nc = T // chunk

    def split(a):
        return jnp.moveaxis(a.reshape(a.shape[:2] + (nc, chunk) + a.shape[3:]), 2, 0)

    def step(carry, xs):
        h, C1, n1, m1 = mlstm_chunk(*xs, *carry)
        return (C1, n1, m1), h

    (C, n, m), h = lax.scan(step, (C, n, m), (split(q), split(k), split(v), split(log_i), split(log_f)))
    h = jnp.moveaxis(h, 0, 2).reshape(B, H, T, -1)
    return h, C, n, m


def mem_attend(qm, mk, mv):
    s = jnp.einsum('bthd,bnhd->bhtn', qm, mk).astype(jnp.float32) * (MEM_HEAD_DIM ** -0.5)
    p = jax.nn.softmax(s, axis=-1).astype(mv.dtype)
    return jnp.einsum('bhtn,bnhd->bthd', p, mv)


def sb_block(q, k, v, q_pos, k_pos):
    z = jnp.einsum('bqhd,bkhd->bhqk', q, k).astype(jnp.float32) * (B_HEAD_DIM ** -0.5)
    mask = k_pos[None, :] < q_pos[:, None]
    log_1m = jnp.where(mask, jax.nn.log_sigmoid(-z), 0.0)
    after = lax.cumsum(log_1m, axis=3, reverse=True) - log_1m
    a = jnp.where(mask, jnp.exp(jax.nn.log_sigmoid(z) + after), 0.0)
    return jnp.einsum('bhqk,bkhd->bqhd', a.astype(v.dtype), v)


def stick_breaking(q, k, v, q_pos, k_pos):
    B, T = q.shape[:2]
    if T <= Q_BLOCK:
        return sb_block(q, k, v, q_pos, k_pos)
    nb = T // Q_BLOCK
    qb = jnp.moveaxis(q.reshape((B, nb, Q_BLOCK) + q.shape[2:]), 1, 0)
    pb = q_pos.reshape(nb, Q_BLOCK)
    out = lax.map(lambda a: sb_block(a[0], k, v, a[1], k_pos), (qb, pb))
    return jnp.moveaxis(out, 0, 1).reshape(q.shape)


def mixer_a(x, w_in, b_gate, w_out, mem_k, mem_v, C, n, m):
    B, T, _ = x.shape
    q, k, v, o, gates, qm = jnp.split(x @ w_in, A_SPLITS, axis=-1)

    def heads(a):
        return a.reshape(B, T, A_HEADS, A_HEAD_DIM).transpose(0, 2, 1, 3).astype(jnp.float32)

    gates = (gates + b_gate).astype(jnp.float32)
    log_i = gates[..., :A_HEADS].transpose(0, 2, 1)
    log_f = jax.nn.log_sigmoid(gates[..., A_HEADS:]).transpose(0, 2, 1)
    h, C, n, m = mlstm_seq(heads(q), heads(k) * (A_HEAD_DIM ** -0.5), heads(v), log_i, log_f,
                           C.astype(jnp.float32), n.astype(jnp.float32), m.astype(jnp.float32))
    h = h.transpose(0, 2, 1, 3).reshape(B, T, MAIN_WIDTH).astype(x.dtype) * jax.nn.sigmoid(o)
    hm = mem_attend(qm.reshape(B, T, MEM_HEADS, MEM_HEAD_DIM), mem_k, mem_v).reshape(B, T, MEM_WIDTH)
    return jnp.concatenate([h, hm], axis=-1) @ w_out, C, n, m


def mixer_b(x, w_in, w_out, k_all, v_all, q_pos, k_pos, mem_k, mem_v):
    B, T, _ = x.shape
    p = x @ w_in
    q = p[..., :MAIN_WIDTH].reshape(B, T, B_HEADS, B_HEAD_DIM)
    qm = p[..., MAIN_WIDTH:].reshape(B, T, MEM_HEADS, MEM_HEAD_DIM)
    h = stick_breaking(q, k_all, v_all, q_pos, k_pos).reshape(B, T, MAIN_WIDTH)
    hm = mem_attend(qm, mem_k, mem_v).reshape(B, T, MEM_WIDTH)
    return jnp.concatenate([h, hm], axis=-1) @ w_out


def moe(x, rw, rb, wgu, bgu, wd, bd):
    logits = (x @ rw + rb).astype(jnp.float32)
    top_val, top_idx = lax.top_k(logits, TOP_K)
    top_w = jax.nn.softmax(top_val, axis=-1)
    gates = jnp.sum(jax.nn.one_hot(top_idx, N_EXPERTS, dtype=jnp.float32) * top_w[..., None], axis=1)
    gates = gates.astype(x.dtype)
    y = jnp.zeros_like(x)
    for e in range(N_EXPERTS):
        hgu = x @ wgu[e] + bgu[e]
        gate = jnp.minimum(hgu[:, :D_FF], SWIGLU_LIMIT)
        up = jnp.clip(hgu[:, D_FF:], -SWIGLU_LIMIT, SWIGLU_LIMIT)
        act = (up + 1.0) * gate * jax.nn.sigmoid(SWIGLU_ALPHA * gate)
        y = y + gates[:, e:e + 1] * (act @ wd[e] + bd[e])
    return y


def trunk(x, C0, n0, m0, past_k, past_v, mem_k, mem_v, w_in_a, b_gate_a, w_in_b, w_kv_b, w_out,
          ln_g, ln_b, router_w, router_b, w_gate_up, b_gate_up, w_down, b_down):
    bsz, t, _ = x.shape
    past = 0 if past_k is None else past_k.shape[1]
    q_pos = past + jnp.arange(t)
    k_pos = jnp.arange(past + t)
    Cs, ns, ms = [], [], []
    k_new = v_new = k_all = v_all = None
    for l in range(DEPTH):
        if l < N_A:
            a, C, n, m = mixer_a(x, w_in_a[l], b_gate_a[l], w_out[l], mem_k[l], mem_v[l], C0[l], n0[l], m0[l])
            Cs.append(C)
            ns.append(n)
            ms.append(m)
        else:
            if l == N_A:
                kv = x @ w_kv_b
                k_new = kv[..., :MAIN_WIDTH].reshape(bsz, t, B_HEADS, B_HEAD_DIM)
                v_new = kv[..., MAIN_WIDTH:].reshape(bsz, t, B_HEADS, B_HEAD_DIM)
                if past_k is None:
                    k_all, v_all = k_new, v_new
                else:
                    k_all = jnp.concatenate([past_k, k_new.astype(past_k.dtype)], axis=1)
                    v_all = jnp.concatenate([past_v, v_new.astype(past_v.dtype)], axis=1)
            a = mixer_b(x, w_in_b[l - N_A], w_out[l], k_all, v_all, q_pos, k_pos, mem_k[l], mem_v[l])
        x = layer_norm(DN_ALPHA * x + a, ln_g[l, 0], ln_b[l, 0])
        f = moe(x.reshape(bsz * t, -1), router_w[l], router_b[l], w_gate_up[l], b_gate_up[l],
                w_down[l], b_down[l]).reshape(x.shape)
        x = layer_norm(DN_ALPHA * x + f, ln_g[l, 1], ln_b[l, 1])
    return x, jnp.stack(Cs), jnp.stack(ns), jnp.stack(ms), k_new, v_new


def setup_inputs(seed: int = 0) -> dict:
    key = jax.random.key(seed)
    ks = jax.random.split(key, 32)
    nrm = jax.random.normal
    f32 = jnp.float32
    gate_bias = jnp.concatenate([0.1 * nrm(ks[0], (N_A, A_HEADS), f32),
                                 3.0 + 0.5 * nrm(ks[1], (N_A, A_HEADS), f32)], axis=-1)
    return {
        'x_prompt': nrm(ks[2], (BATCH, SEQ, D_MODEL), f32),
        'x_sample': nrm(ks[3], (DEC_BATCH, DEC_SEQ, D_MODEL), f32),
        'state_mlstm_C': 0.5 * nrm(ks[4], (N_A, DEC_BATCH, A_HEADS, A_HEAD_DIM, A_HEAD_DIM), f32),
        'state_mlstm_n': 0.5 * nrm(ks[5], (N_A, DEC_BATCH, A_HEADS, A_HEAD_DIM), f32),
        'state_mlstm_m': nrm(ks[6], (N_A, DEC_BATCH, A_HEADS), f32),
        'cache_sb_k': nrm(ks[7], (DEC_BATCH, PAST_LEN, B_HEADS, B_HEAD_DIM), f32),
        'cache_sb_v': nrm(ks[8], (DEC_BATCH, PAST_LEN, B_HEADS, B_HEAD_DIM), f32),
        'cache_mem_k': nrm(ks[9], (DEPTH, DEC_BATCH, N_MEM, MEM_HEADS, MEM_HEAD_DIM), f32),
        'cache_mem_v': nrm(ks[10], (DEPTH, DEC_BATCH, N_MEM, MEM_HEADS, MEM_HEAD_DIM), f32),
        'mem_prompt': nrm(ks[11], (BATCH, N_MEM, D_MODEL), f32),
        'w_in_a': nrm(ks[12], (N_A, D_MODEL, A_IN_WIDTH), f32) * D_MODEL ** -0.5,
        'b_gate_a': gate_bias,
        'w_in_b': nrm(ks[13], (N_B, D_MODEL, B_IN_WIDTH), f32) * D_MODEL ** -0.5,
        'w_kv_b': nrm(ks[14], (D_MODEL, 2 * MAIN_WIDTH), f32) * D_MODEL ** -0.5,
        'w_mem_kv': nrm(ks[15], (DEPTH, D_MODEL, 2 * MEM_WIDTH), f32) * D_MODEL ** -0.5,
        'w_out': nrm(ks[16], (DEPTH, MIX_WIDTH, D_MODEL), f32) * (MIX_WIDTH ** -0.5 * DN_BETA),
        'ln_g': 1.0 + 0.02 * nrm(ks[17], (DEPTH, 2, D_MODEL), f32),
        'ln_b': 0.02 * nrm(ks[18], (DEPTH, 2, D_MODEL), f32),
        'router_w': nrm(ks[19], (DEPTH, D_MODEL, N_EXPERTS), f32) * D_MODEL ** -0.5,
        'router_b': 0.01 * nrm(ks[20], (DEPTH, N_EXPERTS), f32),
        'w_gate_up': nrm(ks[21], (DEPTH, N_EXPERTS, D_MODEL, 2 * D_FF), f32) * D_MODEL ** -0.5,
        'b_gate_up': 0.01 * nrm(ks[22], (DEPTH, N_EXPERTS, 2 * D_FF), f32),
        'w_down': nrm(ks[23], (DEPTH, N_EXPERTS, D_FF, D_MODEL), f32) * (D_FF ** -0.5 * DN_BETA),
        'b_down': 0.01 * nrm(ks[24], (DEPTH, N_EXPERTS, D_MODEL), f32),
    }


def reference(x_prompt, x_sample, state_mlstm_C, state_mlstm_n, state_mlstm_m, cache_sb_k, cache_sb_v,
              cache_mem_k, cache_mem_v, mem_prompt, w_in_a, b_gate_a, w_in_b, w_kv_b, w_mem_kv, w_out,
              ln_g, ln_b, router_w, router_b, w_gate_up, b_gate_up, w_down, b_down):
    bp = x_prompt.shape[0]
    mkv = jnp.einsum('bnd,lde->lbne', mem_prompt, w_mem_kv)
    p_mem_k = mkv[..., :MEM_WIDTH].reshape(DEPTH, bp, N_MEM, MEM_HEADS, MEM_HEAD_DIM)
    p_mem_v = mkv[..., MEM_WIDTH:].reshape(DEPTH, bp, N_MEM, MEM_HEADS, MEM_HEAD_DIM)
    c0 = jnp.zeros((N_A, bp, A_HEADS, A_HEAD_DIM, A_HEAD_DIM), jnp.float32)
    n0 = jnp.zeros((N_A, bp, A_HEADS, A_HEAD_DIM), jnp.float32)
    m0 = jnp.zeros((N_A, bp, A_HEADS), jnp.float32)
    y_prompt, p_C, p_n, p_m, p_sb_k, p_sb_v = trunk(
        x_prompt, c0, n0, m0, None, None, p_mem_k, p_mem_v, w_in_a, b_gate_a, w_in_b, w_kv_b, w_out,
        ln_g, ln_b, router_w, router_b, w_gate_up, b_gate_up, w_down, b_down)
    y_sample, s_C, s_n, s_m, s_sb_k, s_sb_v = trunk(
        x_sample, state_mlstm_C, state_mlstm_n, state_mlstm_m, cache_sb_k, cache_sb_v, cache_mem_k, cache_mem_v,
        w_in_a, b_gate_a, w_in_b, w_kv_b, w_out, ln_g, ln_b, router_w, router_b, w_gate_up, b_gate_up,
        w_down, b_down)
    return (y_prompt, y_sample, p_C, p_n, p_m, p_sb_k, p_sb_v, p_mem_k, p_mem_v,
            s_C, s_n, s_m, s_sb_k, s_sb_v)
```

```python
import functools

import jax
import jax.numpy as jnp
from jax import lax
from jax.experimental import pallas as pl
from jax.experimental.pallas import tpu as pltpu

F32 = jnp.float32
BF16 = jnp.bfloat16
I32 = jnp.int32

D_MODEL = 1024
DEPTH = 4
N_A = DEPTH // 2
MEM_WIDTH = D_MODEL // 4
MAIN_WIDTH = D_MODEL - MEM_WIDTH
MEM_HEADS = 4
MEM_HEAD_DIM = MEM_WIDTH // MEM_HEADS
A_HEADS = 4
A_HEAD_DIM = MAIN_WIDTH // A_HEADS
B_HEADS = 12
B_HEAD_DIM = MAIN_WIDTH // B_HEADS
N_EXPERTS = 32
TOP_K = 4
D_FF = D_MODEL
SWIGLU_LIMIT = 7.0
SWIGLU_ALPHA = 1.702
DN_ALPHA = (2.0 * DEPTH) ** 0.25
LN_EPS = 1e-5

LANES = 128
PACK_ROWS = 16
TOK_TILE = 512
EXP_TILE = 512
MLSTM_CHUNK = 128
VMEM_LIMIT = 56 * 1024 * 1024


def _cparams(*sem):
    return pltpu.CompilerParams(dimension_semantics=sem, vmem_limit_bytes=VMEM_LIMIT)


def _dot(a, b):
    return jnp.dot(a, b, preferred_element_type=F32)


def _dot_nt(a, b):
    return lax.dot_general(a, b, (((1,), (1,)), ((), ())), preferred_element_type=F32)


def _dot_tn(a, b):
    return lax.dot_general(a, b, (((0,), (0,)), ((), ())), preferred_element_type=F32)


def _split_bf16(x):
    hi = x.astype(BF16)
    lo = (x - hi.astype(F32)).astype(BF16)
    return hi, lo


def _softplus(x):
    return jnp.maximum(x, 0.0) + jnp.log(1.0 + jnp.exp(-jnp.abs(x)))


def _mm_kernel(x_ref, w_ref, *o_refs):
    y = _dot(x_ref[...].astype(BF16), w_ref[...])
    for o_ref in o_refs:
        o_ref[...] = y.astype(o_ref.dtype)


def matmul(x, w, *, tm, tn, out_dtypes=(F32,)):
    m, k = x.shape
    n = w.shape[1]
    outs = pl.pallas_call(
        _mm_kernel,
        grid=(m // tm, n // tn),
        in_specs=[pl.BlockSpec((tm, k), lambda i, j: (i, 0)),
                  pl.BlockSpec((k, tn), lambda i, j: (0, j))],
        out_specs=[pl.BlockSpec((tm, tn), lambda i, j: (i, j)) for _ in out_dtypes],
        out_shape=[jax.ShapeDtypeStruct((m, n), dt) for dt in out_dtypes],
        compiler_params=_cparams("parallel", "arbitrary"),
        name="matmul",
    )(x, w)
    return outs


def _proj_a_kernel(x_ref, wg_ref, wqm_ref, wgh_ref, wgl_ref, p_ref, qm_ref, gate_ref):
    x = x_ref[...]
    xh, xl = _split_bf16(x)
    p_ref[0] = _dot(xh, wg_ref[0])

    @pl.when(pl.program_id(1) == 0)
    def _():
        qm_ref[...] = _dot(xh, wqm_ref[...])
        gate_ref[...] = _dot(xh, wgh_ref[...]) + _dot(xl, wgh_ref[...]) + _dot(xh, wgl_ref[...])


def proj_a(x, wg, wqm, wgh, wgl, *, tm):
    m, k = x.shape
    g, _, hd = wg.shape
    return pl.pallas_call(
        _proj_a_kernel,
        grid=(m // tm, g),
        in_specs=[pl.BlockSpec((tm, k), lambda i, j: (i, 0)),
                  pl.BlockSpec((1, k, hd), lambda i, j: (j, 0, 0)),
                  pl.BlockSpec((k, MEM_WIDTH), lambda i, j: (0, 0)),
                  pl.BlockSpec((k, LANES), lambda i, j: (0, 0)),
                  pl.BlockSpec((k, LANES), lambda i, j: (0, 0))],
        out_specs=[pl.BlockSpec((1, tm, hd), lambda i, j: (j, i, 0)),
                   pl.BlockSpec((tm, MEM_WIDTH), lambda i, j: (i, 0)),
                   pl.BlockSpec((tm, LANES), lambda i, j: (i, 0))],
        out_shape=[jax.ShapeDtypeStruct((g, m, hd), F32),
                   jax.ShapeDtypeStruct((m, MEM_WIDTH), F32),
                   jax.ShapeDtypeStruct((m, LANES), F32)],
        compiler_params=_cparams("parallel", "arbitrary"),
        name="proj_a",
    )(x, wg, wqm, wgh, wgl)


def _mlstm_kernel(q_ref, k_ref, v_ref, o_ref, g_ref, c0_ref, nm0_ref,
                  h_ref, cn_ref, nmn_ref, c_s, nm_s, *, chunk, n_chunks, heads, hd):
    step = pl.program_id(1)
    L = chunk

    @pl.when(step == 0)
    def _():
        c_s[...] = c0_ref[0]
        nm_s[...] = nm0_ref[0]

    row = lax.broadcasted_iota(I32, (L, L), 0)
    col = lax.broadcasted_iota(I32, (L, L), 1)
    eye = row == col
    tril = col <= row
    scale = hd ** -0.5

    for ch in range(n_chunks):
        rows = slice(ch * L, (ch + 1) * L)
        for h in range(heads):
            q = q_ref[h, rows, :]
            k = k_ref[h, rows, :] * scale
            v = v_ref[h, rows, :]
            li = g_ref[0, h:h + 1, rows]
            lf = -_softplus(-g_ref[0, heads + h:heads + h + 1, rows])
            lf_b = jnp.broadcast_to(lf, (L, L))
            b_col = jnp.sum(jnp.where(tril, lf_b, 0.0), axis=1, keepdims=True)
            lf_col = jnp.sum(jnp.where(eye, lf_b, 0.0), axis=1, keepdims=True)
            b_row = jnp.sum(jnp.where(row <= col, jnp.broadcast_to(lf_col, (L, L)), 0.0),
                            axis=0, keepdims=True)
            m_prev = nm_s[h, 1:2, 0:1]
            n_row = nm_s[h, 0:1, 0:hd]
            c_old = c_s[h]
            d = jnp.where(tril, b_col - b_row + li, -jnp.inf)
            inter = b_col + m_prev
            m_t = jnp.maximum(inter, jnp.max(d, axis=1, keepdims=True))
            g = jnp.exp(inter - m_t)
            qb = q.astype(BF16)
            kb = k.astype(BF16)
            vb = v.astype(BF16)
            s = _dot_nt(qb, kb) * jnp.exp(d - m_t)
            num = g * _dot(qb, c_old.astype(BF16)) + _dot(s.astype(BF16), vb)
            den = g * jnp.sum(q * n_row, axis=1, keepdims=True) + jnp.sum(s, axis=1, keepdims=True)
            hh = num / jnp.maximum(jnp.abs(den), jnp.exp(-m_t))
            h_ref[h, rows, :] = hh * jax.nn.sigmoid(o_ref[h, rows, :])

            b_last = b_col[L - 1:L, :]
            dec = b_last - b_row + li
            m_new = jnp.maximum(b_last + m_prev, jnp.max(dec, axis=1, keepdims=True))
            g_c = jnp.exp(b_last + m_prev - m_new)
            w_row = jnp.exp(dec - m_new)
            w_col = jnp.sum(jnp.where(eye, jnp.broadcast_to(w_row, (L, L)), 0.0), axis=1, keepdims=True)
            kw = k * w_col
            c_s[h] = g_c * c_old + _dot_tn(kw.astype(BF16), vb)
            nm_s[h, 0:1, 0:hd] = g_c * n_row + jnp.sum(kw, axis=0, keepdims=True)
            nm_s[h, 1:2, :] = jnp.broadcast_to(m_new, (1, nm_s.shape[2]))

    @pl.when(step == pl.num_programs(1) - 1)
    def _():
        cn_ref[0] = c_s[...]
        nmn_ref[0] = nm_s[...]


def mlstm(p, gates, c0, nm0, *, batch, seq, row_base, rows_per_step, chunk):
    hd = p.shape[-1]
    heads = A_HEADS
    steps = seq // rows_per_step
    base = row_base // rows_per_step
    nmw = nm0.shape[-1]

    def tok_map(grp):
        return lambda b, c: (grp, base + b * steps + c, 0)

    kern = functools.partial(_mlstm_kernel, chunk=chunk, n_chunks=rows_per_step // chunk,
                             heads=heads, hd=hd)
    return pl.pallas_call(
        kern,
        grid=(batch, steps),
        in_specs=[pl.BlockSpec((heads, rows_per_step, hd), tok_map(0)),
                  pl.BlockSpec((heads, rows_per_step, hd), tok_map(1)),
                  pl.BlockSpec((heads, rows_per_step, hd), tok_map(2)),
                  pl.BlockSpec((heads, rows_per_step, hd), tok_map(3)),
                  pl.BlockSpec((1, 2 * heads, rows_per_step), lambda b, c: (b, 0, c)),
                  pl.BlockSpec((1, heads, hd, hd), lambda b, c: (b, 0, 0, 0)),
                  pl.BlockSpec((1, heads, 8, nmw), lambda b, c: (b, 0, 0, 0))],
        out_specs=[pl.BlockSpec((heads, rows_per_step, hd), lambda b, c: (0, b * steps + c, 0)),
                   pl.BlockSpec((1, heads, hd, hd), lambda b, c: (b, 0, 0, 0)),
                   pl.BlockSpec((1, heads, 8, nmw), lambda b, c: (b, 0, 0, 0))],
        out_shape=[jax.ShapeDtypeStruct((heads, batch * seq, hd), F32),
                   jax.ShapeDtypeStruct((batch, heads, hd, hd), F32),
                   jax.ShapeDtypeStruct((batch, heads, 8, nmw), F32)],
        scratch_shapes=[pltpu.VMEM((heads, hd, hd), F32), pltpu.VMEM((heads, 8, nmw), F32)],
        compiler_params=_cparams("parallel", "arbitrary"),
        name="mlstm",
    )(p, p, p, p, gates, c0, nm0)


def _mem_kernel(q_ref, mk_ref, mv_ref, o_ref):
    q = q_ref[...]
    mk = mk_ref[0].astype(BF16)
    mv = mv_ref[0].astype(BF16)
    lane = lax.broadcasted_iota(I32, q.shape, 1)
    out = jnp.zeros(q.shape, F32)
    for h in range(MEM_HEADS):
        sel = (lane >= h * MEM_HEAD_DIM) & (lane < (h + 1) * MEM_HEAD_DIM)
        qh = jnp.where(sel, q, 0.0).astype(BF16)
        s = _dot_nt(qh, mk) * (MEM_HEAD_DIM ** -0.5)
        s = s - jnp.max(s, axis=1, keepdims=True)
        e = jnp.exp(s)
        p = e / jnp.sum(e, axis=1, keepdims=True)
        out = jnp.where(sel, _dot(p.astype(BF16), mv), out)
    o_ref[...] = out


def mem_attend(qsrc, col_block, mk, mv, *, batch, seq, row_base, tq):
    steps = seq // tq
    base = row_base // tq
    return pl.pallas_call(
        _mem_kernel,
        grid=(batch, steps),
        in_specs=[pl.BlockSpec((tq, MEM_WIDTH), lambda b, i: (base + b * steps + i, col_block)),
                  pl.BlockSpec((1,) + mk.shape[1:], lambda b, i: (b, 0, 0)),
                  pl.BlockSpec((1,) + mv.shape[1:], lambda b, i: (b, 0, 0))],
        out_specs=pl.BlockSpec((tq, MEM_WIDTH), lambda b, i: (b * steps + i, 0)),
        out_shape=jax.ShapeDtypeStruct((batch * seq, MEM_WIDTH), F32),
        compiler_params=_cparams("parallel", "arbitrary"),
        name="mem_attend",
    )(qsrc, mk, mv)


def _sb_weights(z, carry, tri, mask):
    l1m = -_softplus(z)
    if mask is not None:
        l1m = jnp.where(mask, l1m, 0.0)
    hi, lo = _split_bf16(l1m)
    cin = _dot(hi, tri) + _dot(lo, tri)
    a = jnp.exp(z + carry + cin)
    if mask is not None:
        a = jnp.where(mask, a, 0.0)
    return a, carry + cin[:, 0:1]


def _sbp_kernel(q_ref, k_ref, v_ref, o_ref, acc_s, car_s, *, tq):
    qi = pl.program_id(2)
    q = q_ref[...] * (B_HEAD_DIM ** -0.5)
    lane = lax.broadcasted_iota(I32, q.shape, 1)
    first = lane < B_HEAD_DIM
    qs = (jnp.where(first, q, 0.0).astype(BF16), jnp.where(first, 0.0, q).astype(BF16))
    r = lax.broadcasted_iota(I32, (tq, tq), 0)
    c = lax.broadcasted_iota(I32, (tq, tq), 1)
    tri = jnp.where(r >= c, 1.0, 0.0).astype(BF16)
    causal = c < r

    def block(j, mask):
        start = pl.multiple_of(j * tq, tq)
        kb = k_ref[pl.ds(start, tq), :]
        vb = v_ref[pl.ds(start, tq), :]
        for h in range(2):
            z = _dot_nt(qs[h], kb)
            a, car = _sb_weights(z, car_s[h], tri, mask)
            car_s[h] = car
            acc_s[h] += _dot(a.astype(BF16), vb)

    acc_s[...] = jnp.zeros(acc_s.shape, F32)
    car_s[...] = jnp.zeros(car_s.shape, F32)
    block(qi, causal)

    def body(it, _):
        block(qi - 1 - it, None)
        return 0

    lax.fori_loop(0, qi, body, 0)
    o_ref[...] = jnp.where(first, acc_s[0], acc_s[1])


def sb_prompt(qsrc, kb, vb, *, batch, seq, tq):
    nq = seq // tq
    pairs = MAIN_WIDTH // LANES
    return pl.pallas_call(
        functools.partial(_sbp_kernel, tq=tq),
        grid=(batch, pairs, nq),
        in_specs=[pl.BlockSpec((tq, LANES), lambda b, p, i: (b * nq + i, p)),
                  pl.BlockSpec((seq, LANES), lambda b, p, i: (b, p)),
                  pl.BlockSpec((seq, LANES), lambda b, p, i: (b, p))],
        out_specs=pl.BlockSpec((tq, LANES), lambda b, p, i: (b * nq + i, p)),
        out_shape=jax.ShapeDtypeStruct((batch * seq, MAIN_WIDTH), F32),
        scratch_shapes=[pltpu.VMEM((2, tq, LANES), F32), pltpu.VMEM((2, tq, 1), F32)],
        compiler_params=_cparams("parallel", "parallel", "arbitrary"),
        name="sb_prompt",
    )(qsrc, kb, vb)


def _sbs_kernel(qbd_ref, kn_ref, vn_ref, kc_ref, vc_ref, o_ref, acc_s, car_s, *, tsub, n_sub, tnew):
    j = pl.program_id(1)
    qbd = qbd_ref[0]
    width = qbd.shape[1]

    def block(k, v, tk, mask):
        r = lax.broadcasted_iota(I32, (tk, tk), 0)
        c = lax.broadcasted_iota(I32, (tk, tk), 1)
        tri = jnp.where(c >= r, 1.0, 0.0).astype(BF16)
        z = _dot(k.astype(BF16), qbd)
        l1m = -_softplus(z)
        if mask is not None:
            l1m = jnp.where(mask, l1m, 0.0)
        hi, lo = _split_bf16(l1m)
        cin = _dot(tri, hi) + _dot(tri, lo)
        a = jnp.exp(z + car_s[...] + cin)
        if mask is not None:
            a = jnp.where(mask, a, 0.0)
        car_s[...] += cin[0:1, :]
        acc_s[...] += _dot_tn(a.astype(BF16), v.astype(BF16))

    @pl.when(j == 0)
    def _():
        acc_s[...] = jnp.zeros(acc_s.shape, F32)
        car_s[...] = jnp.zeros(car_s.shape, F32)
        s_idx = lax.broadcasted_iota(I32, (tnew, width), 0)
        t_idx = lax.broadcasted_iota(I32, (tnew, width), 1) % tnew
        block(kn_ref[0], vn_ref[0], tnew, s_idx < t_idx)

    for sub in range(n_sub - 1, -1, -1):
        rows = slice(sub * tsub, (sub + 1) * tsub)
        block(kc_ref[0, rows, :], vc_ref[0, rows, :], tsub, None)

    @pl.when(j == pl.num_programs(1) - 1)
    def _():
        heads = width // tnew
        acc = acc_s[...].reshape(heads, tnew, acc_s.shape[1])
        hrow = lax.broadcasted_iota(I32, acc.shape, 0)
        hcol = lax.broadcasted_iota(I32, acc.shape, 2) // B_HEAD_DIM
        o_ref[0] = jnp.sum(jnp.where(hrow == hcol, acc, 0.0), axis=0)


def sb_sample(qbd, k_new, v_new, k_cache, v_cache, *, tblk, tsub):
    batch, past, width = k_cache.shape
    tnew = k_new.shape[1]
    nblk = past // tblk
    return pl.pallas_call(
        functools.partial(_sbs_kernel, tsub=tsub, n_sub=tblk // tsub, tnew=tnew),
        grid=(batch, nblk),
        in_specs=[pl.BlockSpec((1,) + qbd.shape[1:], lambda b, j: (b, 0, 0)),
                  pl.BlockSpec((1, tnew, width), lambda b, j: (b, 0, 0)),
                  pl.BlockSpec((1, tnew, width), lambda b, j: (b, 0, 0)),
                  pl.BlockSpec((1, tblk, width), lambda b, j: (b, nblk - 1 - j, 0)),
                  pl.BlockSpec((1, tblk, width), lambda b, j: (b, nblk - 1 - j, 0))],
        out_specs=pl.BlockSpec((1, tnew, width), lambda b, j: (b, 0, 0)),
        out_shape=jax.ShapeDtypeStruct((batch, tnew, width), F32),
        scratch_shapes=[pltpu.VMEM((qbd.shape[2], width), F32), pltpu.VMEM((1, qbd.shape[2]), F32)],
        compiler_params=_cparams("parallel", "arbitrary"),
        name="sb_sample",
    )(qbd, k_new, v_new, k_cache, v_cache)


def _layer_norm(x, g, b):
    mu = jnp.mean(x, axis=1, keepdims=True)
    xc = x - mu
    var = jnp.mean(xc * xc, axis=1, keepdims=True)
    return xc * lax.rsqrt(var + LN_EPS) * g + b


def _tail_kernel(*refs, n_main, head_major):
    x_ref = refs[0]
    main_refs = refs[1:1 + n_main]
    hm_ref, wo_ref, wom_ref, g_ref, b_ref, rwh_ref, rwl_ref, rb_ref = refs[1 + n_main:9 + n_main]
    x1_ref, x1b_ref, meta_ref, metat_ref, stat_ref = refs[9 + n_main:]
    t = x_ref.shape[0]

    a = _dot(hm_ref[...].astype(BF16), wom_ref[...])
    if head_major:
        for h in range(n_main):
            a += _dot(main_refs[h][0].astype(BF16), wo_ref[h])
    else:
        a += _dot(main_refs[0][...].astype(BF16), wo_ref[...])
    x1 = _layer_norm(DN_ALPHA * x_ref[...] + a, g_ref[...], b_ref[...])
    x1_ref[...] = x1
    x1b_ref[...] = x1.astype(BF16)

    xh, xl = _split_bf16(x1)
    logits = _dot(xh, rwh_ref[...]) + _dot(xl, rwh_ref[...]) + _dot(xh, rwl_ref[...]) + rb_ref[...]
    lane = lax.broadcasted_iota(I32, (t, LANES), 1)
    logits = jnp.where(lane < N_EXPERTS, logits, -jnp.inf)
    lane_f = lane.astype(F32)
    vals, sels = [], []
    for _ in range(TOP_K):
        mx = jnp.max(logits, axis=1, keepdims=True)
        idx = jnp.min(jnp.where(logits == mx, lane_f, float(LANES)), axis=1, keepdims=True)
        sel = lane_f == idx
        vals.append(mx)
        sels.append(sel)
        logits = jnp.where(sel, -jnp.inf, logits)
    exps = [jnp.exp(v - vals[0]) for v in vals]
    tot = exps[0] + exps[1] + exps[2] + exps[3]
    wts = [e / tot for e in exps]

    onehot = jnp.zeros((t, LANES), F32)
    for sel in sels:
        onehot = jnp.where(sel, 1.0, onehot)
    cnt = jnp.sum(onehot, axis=0, keepdims=True)
    pc = jnp.floor((cnt + (PACK_ROWS - 1)) * (1.0 / PACK_ROWS))
    er = lax.broadcasted_iota(I32, (LANES, LANES), 0)
    ec = lax.broadcasted_iota(I32, (LANES, LANES), 1)
    before = jnp.where(er < ec, 1.0, 0.0).astype(BF16)
    off = _dot(jnp.broadcast_to(pc, (8, LANES)).astype(BF16), before)[0:1, :]
    tr = lax.broadcasted_iota(I32, (t, t), 0)
    tc = lax.broadcasted_iota(I32, (t, t), 1)
    earlier = jnp.where(tc < tr, 1.0, 0.0).astype(BF16)
    rank = _dot(earlier, onehot.astype(BF16))
    posfull = off * float(PACK_ROWS) + rank
    meta = jnp.zeros((t, LANES), F32)
    for k in range(TOP_K):
        pos_k = jnp.sum(jnp.where(sels[k], posfull, 0.0), axis=1, keepdims=True)
        meta = jnp.where(lane == k, pos_k, meta)
        meta = jnp.where(lane == TOP_K + k, wts[k], meta)
    meta_ref[...] = meta
    metat_ref[0] = meta.T[0:8, :]
    srow = lax.broadcasted_iota(I32, (8, LANES), 0)
    stat_ref[0] = jnp.where(srow == 0, pc, jnp.where(srow == 1, off, 0.0))


def mixer_tail(x, mains, hm, wo_main, wo_mem, ln_g, ln_b, rwh, rwl, rb, *, head_major):
    n_tok, d = x.shape
    t = TOK_TILE
    nt = n_tok // t
    n_main = len(mains)
    if head_major:
        hd = mains[0].shape[-1]
        main_specs = [pl.BlockSpec((1, t, hd), functools.partial(lambda h, i: (h, i, 0), h))
                      for h in range(n_main)]
        wo_spec = pl.BlockSpec(wo_main.shape, lambda i: (0, 0, 0))
    else:
        main_specs = [pl.BlockSpec((t, MAIN_WIDTH), lambda i: (i, 0))]
        wo_spec = pl.BlockSpec(wo_main.shape, lambda i: (0, 0))
    const2 = lambda i: (0, 0)
    return pl.pallas_call(
        functools.partial(_tail_kernel, n_main=n_main, head_major=head_major),
        grid=(nt,),
        in_specs=[pl.BlockSpec((t, d), lambda i: (i, 0))] + main_specs + [
            pl.BlockSpec((t, MEM_WIDTH), lambda i: (i, 0)),
            wo_spec,
            pl.BlockSpec(wo_mem.shape, const2),
            pl.BlockSpec((1, d), const2), pl.BlockSpec((1, d), const2),
            pl.BlockSpec((d, LANES), const2), pl.BlockSpec((d, LANES), const2),
            pl.BlockSpec((1, LANES), const2)],
        out_specs=[pl.BlockSpec((t, d), lambda i: (i, 0)),
                   pl.BlockSpec((t, d), lambda i: (i, 0)),
                   pl.BlockSpec((t, LANES), lambda i: (i, 0)),
                   pl.BlockSpec((1, 8, t), lambda i: (i, 0, 0)),
                   pl.BlockSpec((1, 8, LANES), lambda i: (i, 0, 0))],
        out_shape=[jax.ShapeDtypeStruct((n_tok, d), F32),
                   jax.ShapeDtypeStruct((n_tok, d), BF16),
                   jax.ShapeDtypeStruct((n_tok, LANES), F32),
                   jax.ShapeDtypeStruct((nt, 8, t), F32),
                   jax.ShapeDtypeStruct((nt, 8, LANES), F32)],
        compiler_params=_cparams("parallel"),
        name="mixer_tail",
    )(x, *mains, hm, wo_main, wo_mem, ln_g, ln_b, rwh, rwl, rb)


def _tile_rows():
    need = TOK_TILE * TOP_K + N_EXPERTS * (PACK_ROWS - 1)
    return -(-need // 256) * 256


def _piece_copies(pc_ref, off_ref, gd_ref, tile, make_copy, action):
    def per_expert(e, _):
        idx = tile * N_EXPERTS + e
        n = pc_ref[idx]
        src0 = off_ref[idx]
        dst0 = gd_ref[idx]

        def per_piece(p, _):
            s = pl.multiple_of((src0 + p) * PACK_ROWS, PACK_ROWS)
            g = pl.multiple_of((dst0 + p) * PACK_ROWS, PACK_ROWS)
            action(make_copy(s, g))
            return 0

        lax.fori_loop(0, n, per_piece, 0)
        return 0

    lax.fori_loop(0, N_EXPERTS, per_expert, 0)


def _dispatch_kernel(pc_ref, off_ref, gd_ref, metat_ref, x_ref, zero_ref, xg_ref, xs_s, sem):
    del zero_ref
    tile = pl.program_id(0)
    rt, t = xs_s.shape[0], x_ref.shape[0]
    posr = metat_ref[0].astype(I32)
    slot = lax.broadcasted_iota(I32, (rt, t), 0)
    hit = slot == posr[0:1, :]
    for k in range(1, TOP_K):
        hit = hit | (slot == posr[k:k + 1, :])
    xs_s[...] = _dot(jnp.where(hit, 1.0, 0.0).astype(BF16), x_ref[...]).astype(BF16)

    def make_copy(s, g):
        return pltpu.make_async_copy(xs_s.at[pl.ds(s, PACK_ROWS)], xg_ref.at[pl.ds(g, PACK_ROWS)], sem)

    _piece_copies(pc_ref, off_ref, gd_ref, tile, make_copy, lambda cp: cp.start())
    _piece_copies(pc_ref, off_ref, gd_ref, tile, make_copy, lambda cp: cp.wait())


def moe_dispatch(pc, off, gd, metat, x1b, n_rows):
    n_tok, d = x1b.shape
    t = TOK_TILE
    rt = _tile_rows()
    zeros = jnp.zeros((n_rows, d), BF16)
    return pl.pallas_call(
        _dispatch_kernel,
        grid_spec=pltpu.PrefetchScalarGridSpec(
            num_scalar_prefetch=3,
            grid=(n_tok // t,),
            in_specs=[pl.BlockSpec((1, 8, t), lambda i, *_: (i, 0, 0)),
                      pl.BlockSpec((t, d), lambda i, *_: (i, 0)),
                      pl.BlockSpec(memory_space=pl.ANY)],
            out_specs=pl.BlockSpec(memory_space=pl.ANY),
            scratch_shapes=[pltpu.VMEM((rt, d), BF16), pltpu.SemaphoreType.DMA(())]),
        out_shape=jax.ShapeDtypeStruct((n_rows, d), BF16),
        input_output_aliases={5: 0},
        compiler_params=_cparams("arbitrary"),
        name="moe_dispatch",
    )(pc, off, gd, metat, x1b, zeros)


def _expert_kernel(te_ref, nu_ref, x_ref, wgu_ref, bgu_ref, wd_ref, bd_ref, y_ref, wgu_s, wd_s):
    i = pl.program_id(0)
    prev = te_ref[jnp.maximum(i - 1, 0)]

    @pl.when((i == 0) | (te_ref[i] != prev))
    def _():
        wgu_s[...] = wgu_ref[0].astype(BF16)
        wd_s[...] = wd_ref[0].astype(BF16)

    @pl.when(i < nu_ref[0])
    def _():
        x = x_ref[...]
        f = wd_s.shape[0]
        half = f // 2
        y = jnp.broadcast_to(bd_ref[0], y_ref.shape)
        for c in range(2):
            cols = slice(c * half, (c + 1) * half)
            ucols = slice(f + c * half, f + (c + 1) * half)
            gate = jnp.minimum(_dot(x, wgu_s[:, cols]) + bgu_ref[0, :, cols], SWIGLU_LIMIT)
            up = jnp.clip(_dot(x, wgu_s[:, ucols]) + bgu_ref[0, :, ucols], -SWIGLU_LIMIT, SWIGLU_LIMIT)
            act = (up + 1.0) * gate * jax.nn.sigmoid(SWIGLU_ALPHA * gate)
            y = y + _dot(act.astype(BF16), wd_s[cols, :])
        y_ref[...] = y.astype(y_ref.dtype)

    @pl.when(i >= nu_ref[0])
    def _():
        y_ref[...] = jnp.zeros(y_ref.shape, y_ref.dtype)


def moe_experts(te, nused, xg, wgu, bgu, wd, bd):
    n_rows, d = xg.shape
    f2 = wgu.shape[-1]
    tm = EXP_TILE

    def row_map(i, te_ref, nu_ref):
        return (jnp.minimum(i, nu_ref[0] - 1), 0)

    def w_map(i, te_ref, nu_ref):
        return (te_ref[i], 0, 0)

    return pl.pallas_call(
        _expert_kernel,
        grid_spec=pltpu.PrefetchScalarGridSpec(
            num_scalar_prefetch=2,
            grid=(n_rows // tm,),
            in_specs=[pl.BlockSpec((tm, d), row_map),
                      pl.BlockSpec((1, d, f2), w_map),
                      pl.BlockSpec((1, 1, f2), w_map),
                      pl.BlockSpec((1, f2 // 2, d), w_map),
                      pl.BlockSpec((1, 1, d), w_map)],
            out_specs=pl.BlockSpec((tm, d), lambda i, *_: (i, 0)),
            scratch_shapes=[pltpu.VMEM((d, f2), BF16), pltpu.VMEM((f2 // 2, d), BF16)]),
        out_shape=jax.ShapeDtypeStruct((n_rows, d), BF16),
        compiler_params=_cparams("arbitrary"),
        name="moe_experts",
    )(te, nused, xg, wgu, bgu, wd, bd)


def _combine_kernel(pc_ref, off_ref, gd_ref, meta_ref, x1_ref, g_ref, b_ref, yg_ref, x2_ref, ys_s, sem):
    tile = pl.program_id(0)
    rt, t = ys_s.shape[0], x1_ref.shape[0]

    @pl.when(tile == 0)
    def _():
        ys_s[...] = jnp.zeros(ys_s.shape, ys_s.dtype)

    def make_copy(s, g):
        return pltpu.make_async_copy(yg_ref.at[pl.ds(g, PACK_ROWS)], ys_s.at[pl.ds(s, PACK_ROWS)], sem)

    _piece_copies(pc_ref, off_ref, gd_ref, tile, make_copy, lambda cp: cp.start())
    meta = meta_ref[...]
    slot = lax.broadcasted_iota(I32, (t, rt), 1)
    wmat = jnp.zeros((t, rt), F32)
    for k in range(TOP_K):
        wmat = jnp.where(slot == meta[:, k:k + 1].astype(I32), meta[:, TOP_K + k:TOP_K + k + 1], wmat)
    _piece_copies(pc_ref, off_ref, gd_ref, tile, make_copy, lambda cp: cp.wait())
    f = _dot(wmat.astype(BF16), ys_s[...])
    x2_ref[...] = _layer_norm(DN_ALPHA * x1_ref[...] + f, g_ref[...], b_ref[...])


def moe_combine(pc, off, gd, meta, x1, ln_g, ln_b, yg):
    n_tok, d = x1.shape
    t = TOK_TILE
    rt = _tile_rows()
    return pl.pallas_call(
        _combine_kernel,
        grid_spec=pltpu.PrefetchScalarGridSpec(
            num_scalar_prefetch=3,
            grid=(n_tok // t,),
            in_specs=[pl.BlockSpec((t, LANES), lambda i, *_: (i, 0)),
                      pl.BlockSpec((t, d), lambda i, *_: (i, 0)),
                      pl.BlockSpec((1, d), lambda i, *_: (0, 0)),
                      pl.BlockSpec((1, d), lambda i, *_: (0, 0)),
                      pl.BlockSpec(memory_space=pl.ANY)],
            out_specs=pl.BlockSpec((t, d), lambda i, *_: (i, 0)),
            scratch_shapes=[pltpu.VMEM((rt, d), BF16), pltpu.SemaphoreType.DMA(())]),
        out_shape=jax.ShapeDtypeStruct((n_tok, d), F32),
        compiler_params=_cparams("arbitrary"),
        name="moe_combine",
    )(pc, off, gd, meta, x1, ln_g, ln_b, yg)


def _route_tables(stats, n_rows):
    pc = stats[:, 0, :N_EXPERTS].astype(I32)
    off = stats[:, 1, :N_EXPERTS].astype(I32)
    rows_e = jnp.sum(pc, axis=0) * PACK_ROWS
    tiles_e = (rows_e + EXP_TILE - 1) // EXP_TILE
    ends = jnp.cumsum(tiles_e)
    base_e = (ends - tiles_e) * (EXP_TILE // PACK_ROWS)
    gd = base_e[None, :] + jnp.cumsum(pc, axis=0) - pc
    nused = ends[-1]
    tile_ids = jnp.minimum(jnp.arange(n_rows // EXP_TILE, dtype=I32), nused - 1)
    te = jnp.searchsorted(ends, tile_ids, side="right").astype(I32)
    return pc.reshape(-1), off.reshape(-1), gd.reshape(-1).astype(I32), te, nused.reshape(1).astype(I32)


def moe_layer(x1, x1b, meta, metat, stats, ln_g, ln_b, wgu, bgu, wd, bd):
    n_tok = x1.shape[0]
    nt = n_tok // TOK_TILE
    worst = n_tok * TOP_K + nt * N_EXPERTS * (PACK_ROWS - 1) + N_EXPERTS * (EXP_TILE - 1)
    n_rows = -(-worst // EXP_TILE) * EXP_TILE
    pc, off, gd, te, nused = _route_tables(stats, n_rows)
    xg = moe_dispatch(pc, off, gd, metat, x1b, n_rows)
    yg = moe_experts(te, nused, xg, wgu, bgu[:, None, :], wd, bd[:, None, :])
    return moe_combine(pc, off, gd, meta, x1, ln_g, ln_b, yg)


def _pad_cols(w, width):
    return jnp.pad(w, ((0, 0), (0, width - w.shape[1])))


def _hi_lo(w):
    hi = w.astype(BF16)
    return hi, (w - hi.astype(F32)).astype(BF16)


def kernel(x_prompt, x_sample, state_mlstm_C, state_mlstm_n, state_mlstm_m, cache_sb_k, cache_sb_v,
           cache_mem_k, cache_mem_v, mem_prompt, w_in_a, b_gate_a, w_in_b, w_kv_b, w_mem_kv, w_out,
           ln_g, ln_b, router_w, router_b, w_gate_up, b_gate_up, w_down, b_down):
    bp, seq, d = x_prompt.shape
    bs, dseq, _ = x_sample.shape
    n_mem = mem_prompt.shape[1]
    past = cache_sb_k.shape[1]
    n_p = bp * seq
    n_s = bs * dseq
    hd = A_HEAD_DIM
    x = jnp.concatenate([x_prompt.reshape(n_p, d), x_sample.reshape(n_s, d)], axis=0)
    n_tok = n_p + n_s
    tm = n_tok // 26 if n_tok % 26 == 0 and (n_tok // 26) % 8 == 0 else TOK_TILE

    wm = jnp.transpose(w_mem_kv, (1, 0, 2)).reshape(d, DEPTH * 2 * MEM_WIDTH).astype(BF16)
    (mkv,) = matmul(mem_prompt.reshape(bp * n_mem, d), wm, tm=512, tn=512)
    mkv = mkv.reshape(bp, n_mem, DEPTH, 2, MEM_WIDTH)
    p_mem_k = jnp.transpose(mkv[:, :, :, 0], (2, 0, 1, 3))
    p_mem_v = jnp.transpose(mkv[:, :, :, 1], (2, 0, 1, 3))
    s_mem_k = cache_mem_k.reshape(DEPTH, bs, n_mem, MEM_WIDTH)
    s_mem_v = cache_mem_v.reshape(DEPTH, bs, n_mem, MEM_WIDTH)

    def pack_nm(n, m):
        b = n.shape[0]
        out = jnp.zeros((b, A_HEADS, 8, 2 * LANES), F32)
        out = out.at[:, :, 0, :hd].set(n)
        return out.at[:, :, 1, :].set(jnp.broadcast_to(m[:, :, None], (b, A_HEADS, 2 * LANES)))

    cs, ns, ms = [], [], []
    k_new = v_new = None
    for l in range(DEPTH):
        rwh, rwl = _hi_lo(_pad_cols(router_w[l], LANES))
        rb = _pad_cols(router_b[l][None, :], LANES)
        wo = w_out[l].astype(BF16)
        if l < N_A:
            w = w_in_a[l]
            wg = jnp.transpose(w[:, :4 * MAIN_WIDTH].reshape(d, 4 * A_HEADS, hd), (1, 0, 2)).astype(BF16)
            wgh, wgl = _hi_lo(_pad_cols(w[:, 4 * MAIN_WIDTH:4 * MAIN_WIDTH + 2 * A_HEADS], LANES))
            wqm = w[:, 4 * MAIN_WIDTH + 2 * A_HEADS:].astype(BF16)
            p, qm, gates = proj_a(x, wg, wqm, wgh, wgl, tm=tm)
            gates = gates[:, :2 * A_HEADS] + b_gate_a[l][None, :]
            g_p = jnp.transpose(gates[:n_p].reshape(bp, seq, 2 * A_HEADS), (0, 2, 1))
            g_s = jnp.transpose(gates[n_p:].reshape(bs, dseq, 2 * A_HEADS), (0, 2, 1))
            zc = jnp.zeros((bp, A_HEADS, hd, hd), F32)
            znm = jnp.zeros((bp, A_HEADS, 8, 2 * LANES), F32)
            h_p, c_p, nm_p = mlstm(p, g_p, zc, znm, batch=bp, seq=seq, row_base=0,
                                   rows_per_step=4 * MLSTM_CHUNK, chunk=MLSTM_CHUNK)
            h_s, c_s, nm_s = mlstm(p, g_s, state_mlstm_C[l], pack_nm(state_mlstm_n[l], state_mlstm_m[l]),
                                   batch=bs, seq=dseq, row_base=n_p, rows_per_step=dseq, chunk=dseq)
            h_all = jnp.concatenate([h_p, h_s], axis=1)
            cs.append((c_p, c_s))
            ns.append((nm_p[:, :, 0, :hd], nm_s[:, :, 0, :hd]))
            ms.append((nm_p[:, :, 1, 0], nm_s[:, :, 1, 0]))
            hm_p = mem_attend(qm, 0, p_mem_k[l], p_mem_v[l], batch=bp, seq=seq, row_base=0, tq=512)
            hm_s = mem_attend(qm, 0, s_mem_k[l], s_mem_v[l], batch=bs, seq=dseq, row_base=n_p, tq=dseq)
            hm = jnp.concatenate([hm_p, hm_s], axis=0)
            mains = [h_all] * A_HEADS
            wo_main = wo[:MAIN_WIDTH].reshape(A_HEADS, hd, d)
            head_major = True
        else:
            if l == N_A:
                kv32, kv16 = matmul(x, w_kv_b.astype(BF16), tm=tm, tn=512, out_dtypes=(F32, BF16))
                k_new = kv32[:, :MAIN_WIDTH]
                v_new = kv32[:, MAIN_WIDTH:]
                kb = kv16[:, :MAIN_WIDTH]
                vb = kv16[:, MAIN_WIDTH:]
            (pq,) = matmul(x, w_in_b[l - N_A].astype(BF16), tm=tm, tn=512)
            h_p = sb_prompt(pq, kb, vb, batch=bp, seq=seq, tq=256)
            q_s = pq[n_p:, :MAIN_WIDTH].reshape(bs, dseq, B_HEADS, B_HEAD_DIM) * (B_HEAD_DIM ** -0.5)
            qbd = jnp.einsum("bthd,hg->bhdgt", q_s, jnp.eye(B_HEADS, dtype=F32))
            qbd = qbd.reshape(bs, MAIN_WIDTH, B_HEADS * dseq).astype(BF16)
            h_s = sb_sample(qbd, k_new[n_p:].reshape(bs, dseq, MAIN_WIDTH),
                            v_new[n_p:].reshape(bs, dseq, MAIN_WIDTH),
                            cache_sb_k.reshape(bs, past, MAIN_WIDTH), cache_sb_v.reshape(bs, past, MAIN_WIDTH),
                            tblk=1024, tsub=256)
            h_all = jnp.concatenate([h_p, h_s.reshape(n_s, MAIN_WIDTH)], axis=0)
            hm_p = mem_attend(pq, MAIN_WIDTH // MEM_WIDTH, p_mem_k[l], p_mem_v[l],
                              batch=bp, seq=seq, row_base=0, tq=512)
            hm_s = mem_attend(pq, MAIN_WIDTH // MEM_WIDTH, s_mem_k[l], s_mem_v[l],
                              batch=bs, seq=dseq, row_base=n_p, tq=dseq)
            hm = jnp.concatenate([hm_p, hm_s], axis=0)
            mains = [h_all]
            wo_main = wo[:MAIN_WIDTH]
            head_major = False
        x1, x1b, meta, metat, stats = mixer_tail(
            x, mains, hm, wo_main, wo[MAIN_WIDTH:], ln_g[l, 0][None, :], ln_b[l, 0][None, :],
            rwh, rwl, rb, head_major=head_major)
        x = moe_layer(x1, x1b, meta, metat, stats, ln_g[l, 1][None, :], ln_b[l, 1][None, :],
                      w_gate_up[l], b_gate_up[l], w_down[l], b_down[l])

    y_prompt = x[:n_p].reshape(bp, seq, d)
    y_sample = x[n_p:].reshape(bs, dseq, d)
    p_c = jnp.stack([c[0] for c in cs])
    s_c = jnp.stack([c[1] for c in cs])
    p_n = jnp.stack([n[0] for n in ns])
    s_n = jnp.stack([n[1] for n in ns])
    p_m = jnp.stack([m[0] for m in ms])
    s_m = jnp.stack([m[1] for m in ms])
    p_sb_k = k_new[:n_p].reshape(bp, seq, B_HEADS, B_HEAD_DIM)
    p_sb_v = v_new[:n_p].reshape(bp, seq, B_HEADS, B_HEAD_DIM)
    s_sb_k = k_new[n_p:].reshape(bs, dseq, B_HEADS, B_HEAD_DIM)
    s_sb_v = v_new[n_p:].reshape(bs, dseq, B_HEADS, B_HEAD_DIM)
    pmk = p_mem_k.reshape(DEPTH, bp, n_mem, MEM_HEADS, MEM_HEAD_DIM)
    pmv = p_mem_v.reshape(DEPTH, bp, n_mem, MEM_HEADS, MEM_HEAD_DIM)
    return (y_prompt, y_sample, p_c, p_n, p_m, p_sb_k, p_sb_v, pmk, pmv, s_c, s_n, s_m, s_sb_k, s_sb_v)
```

```python
import functools

import jax
import jax.numpy as jnp
from jax import lax
from jax.experimental import pallas as pl
from jax.experimental.pallas import tpu as pltpu

F32 = jnp.float32
BF16 = jnp.bfloat16
I32 = jnp.int32

D_MODEL = 1024
DEPTH = 4
N_A = DEPTH // 2
MEM_WIDTH = D_MODEL // 4
MAIN_WIDTH = D_MODEL - MEM_WIDTH
MEM_HEADS = 4
MEM_HEAD_DIM = MEM_WIDTH // MEM_HEADS
A_HEADS = 4
A_HEAD_DIM = MAIN_WIDTH // A_HEADS
B_HEADS = 12
B_HEAD_DIM = MAIN_WIDTH // B_HEADS
N_EXPERTS = 32
TOP_K = 4
D_FF = D_MODEL
SWIGLU_LIMIT = 7.0
SWIGLU_ALPHA = 1.702
DN_ALPHA = (2.0 * DEPTH) ** 0.25
LN_EPS = 1e-5
LOG2E = 1.4426950408889634

LANES = 128
PACK_ROWS = 16
TOK_TILE = 512
EXP_TILE = 512
MOE_CHUNK = 512
MLSTM_CHUNK = 128
VMEM_LIMIT = 56 * 1024 * 1024


def _cparams(*sem):
    return pltpu.CompilerParams(dimension_semantics=sem, vmem_limit_bytes=VMEM_LIMIT)


def _dot(a, b):
    return jnp.dot(a, b, preferred_element_type=F32)


def _dot_nt(a, b):
    return lax.dot_general(a, b, (((1,), (1,)), ((), ())), preferred_element_type=F32)


def _dot_tn(a, b):
    return lax.dot_general(a, b, (((0,), (0,)), ((), ())), preferred_element_type=F32)


def _split_bf16(x):
    hi = x.astype(BF16)
    lo = (x - hi.astype(F32)).astype(BF16)
    return hi, lo


def _softplus(x):
    return jnp.maximum(x, 0.0) + jnp.log(1.0 + jnp.exp(-jnp.abs(x)))


def _neg_abs(x):
    bits = lax.bitcast_convert_type(x, jnp.uint32) | jnp.uint32(0x80000000)
    return lax.bitcast_convert_type(bits, F32)


def _mm_kernel(x_ref, w_ref, *o_refs):
    y = _dot(x_ref[...].astype(BF16), w_ref[...])
    for o_ref in o_refs:
        o_ref[...] = y.astype(o_ref.dtype)


def matmul(x, w, *, tm, tn, out_dtypes=(F32,)):
    m, k = x.shape
    n = w.shape[1]
    outs = pl.pallas_call(
        _mm_kernel,
        grid=(m // tm, n // tn),
        in_specs=[pl.BlockSpec((tm, k), lambda i, j: (i, 0)),
                  pl.BlockSpec((k, tn), lambda i, j: (0, j))],
        out_specs=[pl.BlockSpec((tm, tn), lambda i, j: (i, j)) for _ in out_dtypes],
        out_shape=[jax.ShapeDtypeStruct((m, n), dt) for dt in out_dtypes],
        compiler_params=_cparams("parallel", "arbitrary"),
        name="matmul",
    )(x, w)
    return outs


def _proj_a_kernel(x_ref, wg_ref, wqm_ref, wgh_ref, wgl_ref, p_ref, qm_ref, gate_ref):
    x = x_ref[...]
    xh, xl = _split_bf16(x)
    for g in range(wg_ref.shape[0]):
        p_ref[g] = _dot(xh, wg_ref[g])

    @pl.when(pl.program_id(1) == 0)
    def _():
        qm_ref[...] = _dot(xh, wqm_ref[...])
        gate_ref[...] = _dot(xh, wgh_ref[...]) + _dot(xl, wgh_ref[...]) + _dot(xh, wgl_ref[...])


def proj_a(x, wg, wqm, wgh, wgl, *, tm):
    m, k = x.shape
    g, _, hd = wg.shape
    gs = A_HEADS
    return pl.pallas_call(
        _proj_a_kernel,
        grid=(m // tm, g // gs),
        in_specs=[pl.BlockSpec((tm, k), lambda i, j: (i, 0)),
                  pl.BlockSpec((gs, k, hd), lambda i, j: (j, 0, 0)),
                  pl.BlockSpec((k, MEM_WIDTH), lambda i, j: (0, 0)),
                  pl.BlockSpec((k, LANES), lambda i, j: (0, 0)),
                  pl.BlockSpec((k, LANES), lambda i, j: (0, 0))],
        out_specs=[pl.BlockSpec((gs, tm, hd), lambda i, j: (j, i, 0)),
                   pl.BlockSpec((tm, MEM_WIDTH), lambda i, j: (i, 0)),
                   pl.BlockSpec((tm, LANES), lambda i, j: (i, 0))],
        out_shape=[jax.ShapeDtypeStruct((g, m, hd), F32),
                   jax.ShapeDtypeStruct((m, MEM_WIDTH), F32),
                   jax.ShapeDtypeStruct((m, LANES), F32)],
        compiler_params=_cparams("parallel", "arbitrary"),
        name="proj_a",
    )(x, wg, wqm, wgh, wgl)


def _mlstm_kernel(q_ref, k_ref, v_ref, o_ref, g_ref, c0_ref, nm0_ref,
                  h_ref, cn_ref, nmn_ref, c_s, nm_s, *, chunk, n_chunks, heads, hd):
    step = pl.program_id(1)
    L = chunk

    @pl.when(step == 0)
    def _():
        c_s[...] = c0_ref[0]
        nm_s[...] = nm0_ref[0]

    row = lax.broadcasted_iota(I32, (L, L), 0)
    col = lax.broadcasted_iota(I32, (L, L), 1)
    eye = row == col
    tril = col <= row
    scale = hd ** -0.5

    for ch in range(n_chunks):
        rows = slice(ch * L, (ch + 1) * L)
        for h in range(heads):
            q = q_ref[h, rows, :]
            k = k_ref[h, rows, :] * scale
            v = v_ref[h, rows, :]
            li = g_ref[0, h:h + 1, rows]
            lf = -_softplus(-g_ref[0, heads + h:heads + h + 1, rows])
            lf_b = jnp.broadcast_to(lf, (L, L))
            b_col = jnp.sum(jnp.where(tril, lf_b, 0.0), axis=1, keepdims=True)
            lf_col = jnp.sum(jnp.where(eye, lf_b, 0.0), axis=1, keepdims=True)
            b_row = jnp.sum(jnp.where(row <= col, jnp.broadcast_to(lf_col, (L, L)), 0.0),
                            axis=0, keepdims=True)
            m_prev = nm_s[h, 1:2, 0:1]
            n_row = nm_s[h, 0:1, 0:hd]
            c_old = c_s[h]
            d = jnp.where(tril, b_col - b_row + li, -jnp.inf)
            inter = b_col + m_prev
            m_t = jnp.maximum(inter, jnp.max(d, axis=1, keepdims=True))
            g = jnp.exp(inter - m_t)
            qb = q.astype(BF16)
            kb = k.astype(BF16)
            vb = v.astype(BF16)
            s = _dot_nt(qb, kb) * jnp.exp(d - m_t)
            num = g * _dot(qb, c_old.astype(BF16)) + _dot(s.astype(BF16), vb)
            den = g * jnp.sum(q * n_row, axis=1, keepdims=True) + jnp.sum(s, axis=1, keepdims=True)
            hh = num / jnp.maximum(jnp.abs(den), jnp.exp(-m_t))
            h_ref[h, rows, :] = hh * jax.nn.sigmoid(o_ref[h, rows, :])

            b_last = b_col[L - 1:L, :]
            dec = b_last - b_row + li
            m_new = jnp.maximum(b_last + m_prev, jnp.max(dec, axis=1, keepdims=True))
            g_c = jnp.exp(b_last + m_prev - m_new)
            w_row = jnp.exp(dec - m_new)
            w_col = jnp.sum(jnp.where(eye, jnp.broadcast_to(w_row, (L, L)), 0.0), axis=1, keepdims=True)
            kw = k * w_col
            c_s[h] = g_c * c_old + _dot_tn(kw.astype(BF16), vb)
            nm_s[h, 0:1, 0:hd] = g_c * n_row + jnp.sum(kw, axis=0, keepdims=True)
            nm_s[h, 1:2, :] = jnp.broadcast_to(m_new, (1, nm_s.shape[2]))

    @pl.when(step == pl.num_programs(1) - 1)
    def _():
        cn_ref[0] = c_s[...]
        nmn_ref[0] = nm_s[...]


def mlstm(p, gates, c0, nm0, *, batch, seq, row_base, rows_per_step, chunk):
    hd = p.shape[-1]
    heads = A_HEADS
    steps = seq // rows_per_step
    base = row_base // rows_per_step
    nmw = nm0.shape[-1]

    def tok_map(grp):
        return lambda b, c: (grp, base + b * steps + c, 0)

    kern = functools.partial(_mlstm_kernel, chunk=chunk, n_chunks=rows_per_step // chunk,
                             heads=heads, hd=hd)
    return pl.pallas_call(
        kern,
        grid=(batch, steps),
        in_specs=[pl.BlockSpec((heads, rows_per_step, hd), tok_map(0)),
                  pl.BlockSpec((heads, rows_per_step, hd), tok_map(1)),
                  pl.BlockSpec((heads, rows_per_step, hd), tok_map(2)),
                  pl.BlockSpec((heads, rows_per_step, hd), tok_map(3)),
                  pl.BlockSpec((1, 2 * heads, rows_per_step), lambda b, c: (b, 0, c)),
                  pl.BlockSpec((1, heads, hd, hd), lambda b, c: (b, 0, 0, 0)),
                  pl.BlockSpec((1, heads, 8, nmw), lambda b, c: (b, 0, 0, 0))],
        out_specs=[pl.BlockSpec((heads, rows_per_step, hd), lambda b, c: (0, b * steps + c, 0)),
                   pl.BlockSpec((1, heads, hd, hd), lambda b, c: (b, 0, 0, 0)),
                   pl.BlockSpec((1, heads, 8, nmw), lambda b, c: (b, 0, 0, 0))],
        out_shape=[jax.ShapeDtypeStruct((heads, batch * seq, hd), F32),
                   jax.ShapeDtypeStruct((batch, heads, hd, hd), F32),
                   jax.ShapeDtypeStruct((batch, heads, 8, nmw), F32)],
        scratch_shapes=[pltpu.VMEM((heads, hd, hd), F32), pltpu.VMEM((heads, 8, nmw), F32)],
        compiler_params=_cparams("parallel", "arbitrary"),
        name="mlstm",
    )(p, p, p, p, gates, c0, nm0)


def _mem_kernel(q_ref, mk_ref, mv_ref, o_ref):
    q = q_ref[...]
    mk = mk_ref[0].astype(BF16)
    mv = mv_ref[0].astype(BF16)
    lane = lax.broadcasted_iota(I32, q.shape, 1)
    out = jnp.zeros(q.shape, F32)
    for h in range(MEM_HEADS):
        sel = (lane >= h * MEM_HEAD_DIM) & (lane < (h + 1) * MEM_HEAD_DIM)
        qh = jnp.where(sel, q, 0.0).astype(BF16)
        s = _dot_nt(qh, mk) * (MEM_HEAD_DIM ** -0.5)
        s = s - jnp.max(s, axis=1, keepdims=True)
        e = jnp.exp(s)
        p = e / jnp.sum(e, axis=1, keepdims=True)
        out = jnp.where(sel, _dot(p.astype(BF16), mv), out)
    o_ref[...] = out


def mem_attend(qsrc, col_block, mk, mv, *, batch, seq, row_base, tq):
    steps = seq // tq
    base = row_base // tq
    return pl.pallas_call(
        _mem_kernel,
        grid=(batch, steps),
        in_specs=[pl.BlockSpec((tq, MEM_WIDTH), lambda b, i: (base + b * steps + i, col_block)),
                  pl.BlockSpec((1,) + mk.shape[1:], lambda b, i: (b, 0, 0)),
                  pl.BlockSpec((1,) + mv.shape[1:], lambda b, i: (b, 0, 0))],
        out_specs=pl.BlockSpec((tq, MEM_WIDTH), lambda b, i: (b * steps + i, 0)),
        out_shape=jax.ShapeDtypeStruct((batch * seq, MEM_WIDTH), F32),
        compiler_params=_cparams("parallel", "arbitrary"),
        name="mem_attend",
    )(qsrc, mk, mv)


def _sbp_kernel(q_ref, k_ref, v_ref, o_ref, w_s, tot_s, car_s, acc_s, *, tq, heads):
    qi = pl.program_id(2)
    qn = q_ref[...] * (-(B_HEAD_DIM ** -0.5) * LOG2E)
    lane = lax.broadcasted_iota(I32, (tq, LANES), 1)
    first = lane < B_HEAD_DIM
    qs = []
    for p in range(heads // 2):
        qp = qn[:, p * LANES:(p + 1) * LANES]
        qs += [jnp.where(first, qp, 0.0).astype(BF16), jnp.where(first, 0.0, qp).astype(BF16)]
    r = lax.broadcasted_iota(I32, (tq, tq), 0)
    c = lax.broadcasted_iota(I32, (tq, tq), 1)
    tri = jnp.where(r >= c, 1.0, 0.0).astype(BF16)
    causal = c < r

    def stage_x(j, masked):
        start = pl.multiple_of(j * tq, tq)
        for h in range(heads):
            cols = slice((h // 2) * LANES, (h // 2 + 1) * LANES)
            zn = _dot_nt(qs[h], k_ref[pl.ds(start, tq), cols])
            l1m = jnp.minimum(zn, 0.0) - jnp.log2(1.0 + jnp.exp2(_neg_abs(zn)))
            if masked:
                l1m = jnp.where(causal, l1m, 0.0)
            cin = _dot(l1m.astype(BF16), tri)
            w = cin - zn
            if masked:
                w = jnp.where(causal, w, -jnp.inf)
            w_s[h] = w
            tot_s[h] = cin[:, 0:1]

    def stage_y(j):
        start = pl.multiple_of(j * tq, tq)
        for h in range(heads):
            cols = slice((h // 2) * LANES, (h // 2 + 1) * LANES)
            car = car_s[h]
            a = jnp.exp2(w_s[h] + car)
            acc_s[h] += _dot(a.astype(BF16), v_ref[pl.ds(start, tq), cols])
            car_s[h] = car + tot_s[h]

    acc_s[...] = jnp.zeros(acc_s.shape, F32)
    car_s[...] = jnp.zeros(car_s.shape, F32)
    stage_x(qi, True)

    def body(it, _):
        j = qi - 1 - it
        stage_y(j + 1)
        stage_x(j, False)
        return 0

    lax.fori_loop(0, qi, body, 0)
    stage_y(0)
    for p in range(heads // 2):
        o_ref[:, p * LANES:(p + 1) * LANES] = jnp.where(first, acc_s[2 * p], acc_s[2 * p + 1])


def sb_prompt(qsrc, kv, *, batch, seq, tq, heads):
    nq = seq // tq
    width = heads * B_HEAD_DIM
    groups = MAIN_WIDTH // width
    return pl.pallas_call(
        functools.partial(_sbp_kernel, tq=tq, heads=heads),
        grid=(batch, groups, nq),
        in_specs=[pl.BlockSpec((tq, width), lambda b, p, i: (b * nq + i, p)),
                  pl.BlockSpec((seq, width), lambda b, p, i: (b, p)),
                  pl.BlockSpec((seq, width), lambda b, p, i: (b, groups + p))],
        out_specs=pl.BlockSpec((tq, width), lambda b, p, i: (b * nq + i, p)),
        out_shape=jax.ShapeDtypeStruct((batch * seq, MAIN_WIDTH), F32),
        scratch_shapes=[pltpu.VMEM((heads, tq, tq), F32), pltpu.VMEM((heads, tq, 1), F32),
                        pltpu.VMEM((heads, tq, 1), F32), pltpu.VMEM((heads, tq, LANES), F32)],
        compiler_params=_cparams("parallel", "parallel", "arbitrary"),
        name="sb_prompt",
    )(qsrc, kv, kv)


def _sbs_kernel(qbd_ref, kn_ref, vn_ref, kc_ref, vc_ref, o_ref, acc_s, car_s, *, tsub, n_sub, tnew):
    j = pl.program_id(1)
    qbd = qbd_ref[0]
    width = qbd.shape[1]

    def block(k, v, tk, mask):
        r = lax.broadcasted_iota(I32, (tk, tk), 0)
        c = lax.broadcasted_iota(I32, (tk, tk), 1)
        tri = jnp.where(c >= r, 1.0, 0.0).astype(BF16)
        z = _dot(k.astype(BF16), qbd)
        l1m = -_softplus(z)
        if mask is not None:
            l1m = jnp.where(mask, l1m, 0.0)
        hi, lo = _split_bf16(l1m)
        cin = _dot(tri, hi) + _dot(tri, lo)
        a = jnp.exp(z + car_s[...] + cin)
        if mask is not None:
            a = jnp.where(mask, a, 0.0)
        car_s[...] += cin[0:1, :]
        acc_s[...] += _dot_tn(a.astype(BF16), v.astype(BF16))

    @pl.when(j == 0)
    def _():
        acc_s[...] = jnp.zeros(acc_s.shape, F32)
        car_s[...] = jnp.zeros(car_s.shape, F32)
        s_idx = lax.broadcasted_iota(I32, (tnew, width), 0)
        t_idx = lax.broadcasted_iota(I32, (tnew, width), 1) % tnew
        block(kn_ref[0], vn_ref[0], tnew, s_idx < t_idx)

    for sub in range(n_sub - 1, -1, -1):
        rows = slice(sub * tsub, (sub + 1) * tsub)
        block(kc_ref[0, rows, :], vc_ref[0, rows, :], tsub, None)

    @pl.when(j == pl.num_programs(1) - 1)
    def _():
        heads = width // tnew
        acc = acc_s[...].reshape(heads, tnew, acc_s.shape[1])
        hrow = lax.broadcasted_iota(I32, acc.shape, 0)
        hcol = lax.broadcasted_iota(I32, acc.shape, 2) // B_HEAD_DIM
        o_ref[0] = jnp.sum(jnp.where(hrow == hcol, acc, 0.0), axis=0)


def sb_sample(qbd, k_new, v_new, k_cache, v_cache, *, tblk, tsub):
    batch, past, width = k_cache.shape
    tnew = k_new.shape[1]
    nblk = past // tblk
    return pl.pallas_call(
        functools.partial(_sbs_kernel, tsub=tsub, n_sub=tblk // tsub, tnew=tnew),
        grid=(batch, nblk),
        in_specs=[pl.BlockSpec((1,) + qbd.shape[1:], lambda b, j: (b, 0, 0)),
                  pl.BlockSpec((1, tnew, width), lambda b, j: (b, 0, 0)),
                  pl.BlockSpec((1, tnew, width), lambda b, j: (b, 0, 0)),
                  pl.BlockSpec((1, tblk, width), lambda b, j: (b, nblk - 1 - j, 0)),
                  pl.BlockSpec((1, tblk, width), lambda b, j: (b, nblk - 1 - j, 0))],
        out_specs=pl.BlockSpec((1, tnew, width), lambda b, j: (b, 0, 0)),
        out_shape=jax.ShapeDtypeStruct((batch, tnew, width), F32),
        scratch_shapes=[pltpu.VMEM((qbd.shape[2], width), F32), pltpu.VMEM((1, qbd.shape[2]), F32)],
        compiler_params=_cparams("parallel", "arbitrary"),
        name="sb_sample",
    )(qbd, k_new, v_new, k_cache, v_cache)


def _layer_norm(x, g, b):
    mu = jnp.mean(x, axis=1, keepdims=True)
    xc = x - mu
    var = jnp.mean(xc * xc, axis=1, keepdims=True)
    return xc * lax.rsqrt(var + LN_EPS) * g + b


def _tail_kernel(*refs, n_main, head_major, prompt_tiles):
    x_ref = refs[0]
    mainp_refs = refs[1:1 + n_main]
    mains_refs = refs[1 + n_main:1 + 2 * n_main]
    rest = refs[1 + 2 * n_main:]
    hmp_ref, hms_ref, wo_ref, wom_ref, g_ref, b_ref, rwh_ref, rwl_ref, rb_ref = rest[:9]
    x1_ref, x1b_ref, meta_ref, metat_ref, stat_ref = rest[9:]
    t = x_ref.shape[0]
    is_sample = pl.program_id(0) >= prompt_tiles

    def pick(p_ref, s_ref):
        if head_major:
            return jnp.where(is_sample, s_ref[0], p_ref[0]).astype(BF16)
        return jnp.where(is_sample, s_ref[...], p_ref[...]).astype(BF16)

    a = _dot(jnp.where(is_sample, hms_ref[...], hmp_ref[...]).astype(BF16), wom_ref[...])
    if head_major:
        for h in range(n_main):
            a += _dot(pick(mainp_refs[h], mains_refs[h]), wo_ref[h])
    else:
        a += _dot(pick(mainp_refs[0], mains_refs[0]), wo_ref[...])
    x1 = _layer_norm(DN_ALPHA * x_ref[...] + a, g_ref[...], b_ref[...])
    x1_ref[...] = x1
    x1b_ref[...] = x1.astype(BF16)

    xh, xl = _split_bf16(x1)
    logits = _dot(xh, rwh_ref[...]) + _dot(xl, rwh_ref[...]) + _dot(xh, rwl_ref[...]) + rb_ref[...]
    lane = lax.broadcasted_iota(I32, (t, LANES), 1)
    logits = jnp.where(lane < N_EXPERTS, logits, -jnp.inf)
    lane_f = lane.astype(F32)
    vals, sels = [], []
    for _ in range(TOP_K):
        mx = jnp.max(logits, axis=1, keepdims=True)
        idx = jnp.min(jnp.where(logits == mx, lane_f, float(LANES)), axis=1, keepdims=True)
        sel = lane_f == idx
        vals.append(mx)
        sels.append(sel)
        logits = jnp.where(sel, -jnp.inf, logits)
    exps = [jnp.exp(v - vals[0]) for v in vals]
    tot = exps[0] + exps[1] + exps[2] + exps[3]
    wts = [e / tot for e in exps]

    onehot = jnp.zeros((t, LANES), F32)
    for sel in sels:
        onehot = jnp.where(sel, 1.0, onehot)
    cnt = jnp.sum(onehot, axis=0, keepdims=True)
    pc = jnp.floor((cnt + (PACK_ROWS - 1)) * (1.0 / PACK_ROWS))
    er = lax.broadcasted_iota(I32, (LANES, LANES), 0)
    ec = lax.broadcasted_iota(I32, (LANES, LANES), 1)
    before = jnp.where(er < ec, 1.0, 0.0).astype(BF16)
    off = _dot(jnp.broadcast_to(pc, (8, LANES)).astype(BF16), before)[0:1, :]
    tr = lax.broadcasted_iota(I32, (t, t), 0)
    tc = lax.broadcasted_iota(I32, (t, t), 1)
    earlier = jnp.where(tc < tr, 1.0, 0.0).astype(BF16)
    rank = _dot(earlier, onehot.astype(BF16))
    posfull = off * float(PACK_ROWS) + rank
    meta = jnp.zeros((t, LANES), F32)
    for k in range(TOP_K):
        pos_k = jnp.sum(jnp.where(sels[k], posfull, 0.0), axis=1, keepdims=True)
        meta = jnp.where(lane == k, pos_k, meta)
        meta = jnp.where(lane == TOP_K + k, wts[k], meta)
    meta_ref[...] = meta
    metat_ref[0] = meta.T[0:8, :]
    srow = lax.broadcasted_iota(I32, (8, LANES), 0)
    stat_ref[0] = jnp.where(srow == 0, pc, jnp.where(srow == 1, off, 0.0))


def mixer_tail(x, main_p, main_s, hm_p, hm_s, wo_main, wo_mem, ln_g, ln_b, rwh, rwl, rb, *, head_major):
    n_tok, d = x.shape
    t = TOK_TILE
    nt = n_tok // t
    assert hm_s.shape[0] == t and hm_p.shape[0] == (nt - 1) * t
    last_p = nt - 2
    if head_major:
        n_main, _, hd = main_p.shape
        mainp_specs = [pl.BlockSpec((1, t, hd), functools.partial(lambda h, i: (h, jnp.minimum(i, last_p), 0), h))
                       for h in range(n_main)]
        mains_specs = [pl.BlockSpec((1, t, hd), functools.partial(lambda h, i: (h, 0, 0), h))
                       for h in range(n_main)]
        wo_spec = pl.BlockSpec(wo_main.shape, lambda i: (0, 0, 0))
    else:
        n_main = 1
        mainp_specs = [pl.BlockSpec((t, MAIN_WIDTH), lambda i: (jnp.minimum(i, last_p), 0))]
        mains_specs = [pl.BlockSpec((t, MAIN_WIDTH), lambda i: (0, 0))]
        wo_spec = pl.BlockSpec(wo_main.shape, lambda i: (0, 0))
    const2 = lambda i: (0, 0)
    return pl.pallas_call(
        functools.partial(_tail_kernel, n_main=n_main, head_major=head_major, prompt_tiles=nt - 1),
        grid=(nt,),
        in_specs=[pl.BlockSpec((t, d), lambda i: (i, 0))] + mainp_specs + mains_specs + [
            pl.BlockSpec((t, MEM_WIDTH), lambda i: (jnp.minimum(i, last_p), 0)),
            pl.BlockSpec((t, MEM_WIDTH), const2),
            wo_spec,
            pl.BlockSpec(wo_mem.shape, const2),
            pl.BlockSpec((1, d), const2), pl.BlockSpec((1, d), const2),
            pl.BlockSpec((d, LANES), const2), pl.BlockSpec((d, LANES), const2),
            pl.BlockSpec((1, LANES), const2)],
        out_specs=[pl.BlockSpec((t, d), lambda i: (i, 0)),
                   pl.BlockSpec((t, d), lambda i: (i, 0)),
                   pl.BlockSpec((t, LANES), lambda i: (i, 0)),
                   pl.BlockSpec((1, 8, t), lambda i: (i, 0, 0)),
                   pl.BlockSpec((1, 8, LANES), lambda i: (i, 0, 0))],
        out_shape=[jax.ShapeDtypeStruct((n_tok, d), F32),
                   jax.ShapeDtypeStruct((n_tok, d), BF16),
                   jax.ShapeDtypeStruct((n_tok, LANES), F32),
                   jax.ShapeDtypeStruct((nt, 8, t), F32),
                   jax.ShapeDtypeStruct((nt, 8, LANES), F32)],
        compiler_params=_cparams("parallel"),
        name="mixer_tail",
    )(x, *([main_p] * n_main), *([main_s] * n_main), hm_p, hm_s, wo_main, wo_mem, ln_g, ln_b, rwh, rwl, rb)


def _tile_rows():
    need = TOK_TILE * TOP_K + N_EXPERTS * (PACK_ROWS - 1)
    return -(-need // 256) * 256


def _piece_copies(dst_ref, np_ref, tile, make_copy, action):
    pieces = _tile_rows() // PACK_ROWS

    def per_piece(p, _):
        s = pl.multiple_of(p * PACK_ROWS, PACK_ROWS)
        g = pl.multiple_of(dst_ref[tile * pieces + p] * PACK_ROWS, PACK_ROWS)
        action(make_copy(s, g))
        return 0

    lax.fori_loop(0, np_ref[tile], per_piece, 0)


def _zero_fill(ts_ref, tn_ref, nu_ref, xg_ref, zero_s, sem, action):
    def per_expert(e, _):
        def per_piece(p, _):
            g = pl.multiple_of((ts_ref[e] + p) * PACK_ROWS, PACK_ROWS)
            action(pltpu.make_async_copy(zero_s.at[pl.ds(0, PACK_ROWS)], xg_ref.at[pl.ds(g, PACK_ROWS)], sem))
            return 0

        lax.fori_loop(0, tn_ref[e], per_piece, 0)
        return 0

    lax.fori_loop(0, N_EXPERTS, per_expert, 0)

    def per_tile(i, _):
        g = pl.multiple_of(i * EXP_TILE, EXP_TILE)
        action(pltpu.make_async_copy(zero_s, xg_ref.at[pl.ds(g, EXP_TILE)], sem))
        return 0

    lax.fori_loop(nu_ref[0], xg_ref.shape[0] // EXP_TILE, per_tile, 0)


def _dispatch_kernel(dst_ref, np_ref, ts_ref, tn_ref, nu_ref, metat_ref, x_ref, xg_ref,
                     xs_s, zero_s, sem, zsem):
    tile = pl.program_id(0)
    rt, t = xs_s.shape[0], x_ref.shape[0]

    @pl.when(tile == 0)
    def _():
        zero_s[...] = jnp.zeros(zero_s.shape, zero_s.dtype)
        _zero_fill(ts_ref, tn_ref, nu_ref, xg_ref, zero_s, zsem, lambda cp: cp.start())
        _zero_fill(ts_ref, tn_ref, nu_ref, xg_ref, zero_s, zsem, lambda cp: cp.wait())

    posr = metat_ref[0]
    x = x_ref[...]
    for c in range(rt // MOE_CHUNK):
        slot = (lax.broadcasted_iota(I32, (MOE_CHUNK, t), 0) + c * MOE_CHUNK).astype(F32)
        hit = slot == posr[0:1, :]
        for k in range(1, TOP_K):
            hit = hit | (slot == posr[k:k + 1, :])
        rows = slice(c * MOE_CHUNK, (c + 1) * MOE_CHUNK)
        xs_s[rows, :] = _dot(jnp.where(hit, 1.0, 0.0).astype(BF16), x).astype(BF16)

    def make_copy(s, g):
        return pltpu.make_async_copy(xs_s.at[pl.ds(s, PACK_ROWS)], xg_ref.at[pl.ds(g, PACK_ROWS)], sem)

    _piece_copies(dst_ref, np_ref, tile, make_copy, lambda cp: cp.start())
    _piece_copies(dst_ref, np_ref, tile, make_copy, lambda cp: cp.wait())


def moe_dispatch(dstp, npieces, tail_start, tail_n, nused, metat, x1b, n_rows):
    n_tok, d = x1b.shape
    t = TOK_TILE
    rt = _tile_rows()
    return pl.pallas_call(
        _dispatch_kernel,
        grid_spec=pltpu.PrefetchScalarGridSpec(
            num_scalar_prefetch=5,
            grid=(n_tok // t,),
            in_specs=[pl.BlockSpec((1, 8, t), lambda i, *_: (i, 0, 0)),
                      pl.BlockSpec((t, d), lambda i, *_: (i, 0))],
            out_specs=pl.BlockSpec(memory_space=pl.ANY),
            scratch_shapes=[pltpu.VMEM((rt, d), BF16), pltpu.VMEM((EXP_TILE, d), BF16),
                            pltpu.SemaphoreType.DMA(()), pltpu.SemaphoreType.DMA(())]),
        out_shape=jax.ShapeDtypeStruct((n_rows, d), BF16),
        compiler_params=_cparams("arbitrary"),
        name="moe_dispatch",
    )(dstp, npieces, tail_start, tail_n, nused, metat, x1b)


def _expert_kernel(te_ref, nu_ref, x_ref, wgu_ref, bgu_ref, wd_ref, bd_ref, y_ref, wgu_s, wd_s):
    i = pl.program_id(0)
    prev = te_ref[jnp.maximum(i - 1, 0)]

    @pl.when((i == 0) | (te_ref[i] != prev))
    def _():
        wgu_s[...] = wgu_ref[0].astype(BF16)
        wd_s[...] = wd_ref[0].astype(BF16)

    @pl.when(i < nu_ref[0])
    def _():
        x = x_ref[...]
        f = wd_s.shape[0]
        half = f // 2
        y = jnp.broadcast_to(bd_ref[0], y_ref.shape)
        for c in range(2):
            cols = slice(c * half, (c + 1) * half)
            ucols = slice(f + c * half, f + (c + 1) * half)
            gate = jnp.minimum(_dot(x, wgu_s[:, cols]) + bgu_ref[0, :, cols], SWIGLU_LIMIT)
            up = jnp.clip(_dot(x, wgu_s[:, ucols]) + bgu_ref[0, :, ucols], -SWIGLU_LIMIT, SWIGLU_LIMIT)
            act = (up + 1.0) * gate * jax.nn.sigmoid(SWIGLU_ALPHA * gate)
            y = y + _dot(act.astype(BF16), wd_s[cols, :])
        y_ref[...] = y.astype(y_ref.dtype)

    @pl.when(i >= nu_ref[0])
    def _():
        y_ref[...] = jnp.zeros(y_ref.shape, y_ref.dtype)


def moe_experts(te, nused, xg, wgu, bgu, wd, bd):
    n_rows, d = xg.shape
    f2 = wgu.shape[-1]
    tm = EXP_TILE

    def row_map(i, te_ref, nu_ref):
        return (jnp.minimum(i, nu_ref[0] - 1), 0)

    def w_map(i, te_ref, nu_ref):
        return (te_ref[i], 0, 0)

    return pl.pallas_call(
        _expert_kernel,
        grid_spec=pltpu.PrefetchScalarGridSpec(
            num_scalar_prefetch=2,
            grid=(n_rows // tm,),
            in_specs=[pl.BlockSpec((tm, d), row_map),
                      pl.BlockSpec((1, d, f2), w_map),
                      pl.BlockSpec((1, 1, f2), w_map),
                      pl.BlockSpec((1, f2 // 2, d), w_map),
                      pl.BlockSpec((1, 1, d), w_map)],
            out_specs=pl.BlockSpec((tm, d), lambda i, *_: (i, 0)),
            scratch_shapes=[pltpu.VMEM((d, f2), BF16), pltpu.VMEM((f2 // 2, d), BF16)]),
        out_shape=jax.ShapeDtypeStruct((n_rows, d), BF16),
        compiler_params=_cparams("arbitrary"),
        name="moe_experts",
    )(te, nused, xg, wgu, bgu, wd, bd)


def _combine_kernel(dst_ref, np_ref, meta_ref, x1_ref, g_ref, b_ref, yg_ref, x2_ref, ys_s, sem):
    tile = pl.program_id(0)
    rt, t = ys_s.shape[0], x1_ref.shape[0]

    @pl.when(tile == 0)
    def _():
        ys_s[...] = jnp.zeros(ys_s.shape, ys_s.dtype)

    def make_copy(s, g):
        return pltpu.make_async_copy(yg_ref.at[pl.ds(g, PACK_ROWS)], ys_s.at[pl.ds(s, PACK_ROWS)], sem)

    _piece_copies(dst_ref, np_ref, tile, make_copy, lambda cp: cp.start())
    _piece_copies(dst_ref, np_ref, tile, make_copy, lambda cp: cp.wait())
    meta = meta_ref[...]
    f = jnp.zeros(x1_ref.shape, F32)
    for c in range(rt // MOE_CHUNK):
        slot = (lax.broadcasted_iota(I32, (t, MOE_CHUNK), 1) + c * MOE_CHUNK).astype(F32)
        wmat = jnp.zeros((t, MOE_CHUNK), F32)
        for k in range(TOP_K):
            wmat = jnp.where(slot == meta[:, k:k + 1], meta[:, TOP_K + k:TOP_K + k + 1], wmat)
        f += _dot(wmat.astype(BF16), ys_s[c * MOE_CHUNK:(c + 1) * MOE_CHUNK, :])
    x2_ref[...] = _layer_norm(DN_ALPHA * x1_ref[...] + f, g_ref[...], b_ref[...])


def moe_combine(dstp, npieces, meta, x1, ln_g, ln_b, yg):
    n_tok, d = x1.shape
    t = TOK_TILE
    rt = _tile_rows()
    return pl.pallas_call(
        _combine_kernel,
        grid_spec=pltpu.PrefetchScalarGridSpec(
            num_scalar_prefetch=2,
            grid=(n_tok // t,),
            in_specs=[pl.BlockSpec((t, LANES), lambda i, *_: (i, 0)),
                      pl.BlockSpec((t, d), lambda i, *_: (i, 0)),
                      pl.BlockSpec((1, d), lambda i, *_: (0, 0)),
                      pl.BlockSpec((1, d), lambda i, *_: (0, 0)),
                      pl.BlockSpec(memory_space=pl.ANY)],
            out_specs=pl.BlockSpec((t, d), lambda i, *_: (i, 0)),
            scratch_shapes=[pltpu.VMEM((rt, d), BF16), pltpu.SemaphoreType.DMA(())]),
        out_shape=jax.ShapeDtypeStruct((n_tok, d), F32),
        compiler_params=_cparams("arbitrary"),
        name="moe_combine",
    )(dstp, npieces, meta, x1, ln_g, ln_b, yg)


def _route_tables(stats, n_rows):
    pc = stats[:, 0, :N_EXPERTS].astype(I32)
    off = stats[:, 1, :N_EXPERTS].astype(I32)
    pieces_e = jnp.sum(pc, axis=0)
    per_tile = EXP_TILE // PACK_ROWS
    tiles_e = (pieces_e + per_tile - 1) // per_tile
    ends = jnp.cumsum(tiles_e)
    base_e = (ends - tiles_e) * per_tile
    gd = base_e[None, :] + jnp.cumsum(pc, axis=0) - pc
    tail_start = base_e + pieces_e
    tail_n = tiles_e * per_tile - pieces_e
    nused = ends[-1]
    tile_ids = jnp.minimum(jnp.arange(n_rows // EXP_TILE, dtype=I32), nused - 1)
    te = jnp.sum((ends[None, :] <= tile_ids[:, None]).astype(I32), axis=1)
    seg_end = off + pc
    pidx = jnp.arange(_tile_rows() // PACK_ROWS, dtype=I32)
    e_of_p = jnp.sum((seg_end[:, None, :] <= pidx[None, :, None]).astype(I32), axis=2)
    sel = e_of_p[:, :, None] == jnp.arange(N_EXPERTS, dtype=I32)[None, None, :]
    dstp = jnp.sum(jnp.where(sel, (gd - off)[:, None, :], 0), axis=2) + pidx[None, :]
    npieces = seg_end[:, -1]
    return (dstp.reshape(-1).astype(I32), npieces.astype(I32), tail_start.astype(I32),
            tail_n.astype(I32), te, nused.reshape(1).astype(I32))


def moe_layer(x1, x1b, meta, metat, stats, ln_g, ln_b, wgu, bgu, wd, bd, layer):
    n_tok = x1.shape[0]
    nt = n_tok // TOK_TILE
    worst = n_tok * TOP_K + nt * N_EXPERTS * (PACK_ROWS - 1) + N_EXPERTS * (EXP_TILE - 1)
    n_rows = -(-worst // EXP_TILE) * EXP_TILE
    dstp, npieces, tail_start, tail_n, te, nused = _route_tables(stats, n_rows)
    xg = moe_dispatch(dstp, npieces, tail_start, tail_n, nused, metat, x1b, n_rows)
    yg = moe_experts(te + layer * N_EXPERTS, nused, xg, wgu, bgu, wd, bd)
    return moe_combine(dstp, npieces, meta, x1, ln_g, ln_b, yg)


def _pad_cols(w, width):
    return jnp.pad(w, ((0, 0), (0, width - w.shape[1])))


def _hi_lo(w):
    hi = w.astype(BF16)
    return hi, (w - hi.astype(F32)).astype(BF16)


def kernel(x_prompt, x_sample, state_mlstm_C, state_mlstm_n, state_mlstm_m, cache_sb_k, cache_sb_v,
           cache_mem_k, cache_mem_v, mem_prompt, w_in_a, b_gate_a, w_in_b, w_kv_b, w_mem_kv, w_out,
           ln_g, ln_b, router_w, router_b, w_gate_up, b_gate_up, w_down, b_down):
    bp, seq, d = x_prompt.shape
    bs, dseq, _ = x_sample.shape
    n_mem = mem_prompt.shape[1]
    past = cache_sb_k.shape[1]
    n_p = bp * seq
    n_s = bs * dseq
    hd = A_HEAD_DIM
    x = jnp.concatenate([x_prompt.reshape(n_p, d), x_sample.reshape(n_s, d)], axis=0)
    n_tok = n_p + n_s
    tm = n_tok // 26 if n_tok % 26 == 0 and (n_tok // 26) % 8 == 0 else TOK_TILE

    wm = jnp.transpose(w_mem_kv, (1, 0, 2)).reshape(d, DEPTH * 2 * MEM_WIDTH).astype(BF16)
    (mkv,) = matmul(mem_prompt.reshape(bp * n_mem, d), wm, tm=512, tn=512)
    mkv = mkv.reshape(bp, n_mem, DEPTH, 2, MEM_WIDTH)
    p_mem_k = jnp.transpose(mkv[:, :, :, 0], (2, 0, 1, 3))
    p_mem_v = jnp.transpose(mkv[:, :, :, 1], (2, 0, 1, 3))
    s_mem_k = cache_mem_k.reshape(DEPTH, bs, n_mem, MEM_WIDTH)
    s_mem_v = cache_mem_v.reshape(DEPTH, bs, n_mem, MEM_WIDTH)

    def pack_nm(n, m):
        b = n.shape[0]
        out = jnp.zeros((b, A_HEADS, 8, 2 * LANES), F32)
        out = out.at[:, :, 0, :hd].set(n)
        return out.at[:, :, 1, :].set(jnp.broadcast_to(m[:, :, None], (b, A_HEADS, 2 * LANES)))

    k_cache = cache_sb_k.reshape(bs, past, MAIN_WIDTH)
    v_cache = cache_sb_v.reshape(bs, past, MAIN_WIDTH)
    n_exp = DEPTH * N_EXPERTS
    wgu_all = w_gate_up.reshape(n_exp, d, 2 * D_FF)
    bgu_all = b_gate_up.reshape(n_exp, 1, 2 * D_FF)
    wd_all = w_down.reshape(n_exp, D_FF, d)
    bd_all = b_down.reshape(n_exp, 1, d)

    cs, ns, ms = [], [], []
    k_new = v_new = kv16 = None
    for l in range(DEPTH):
        rwh, rwl = _hi_lo(_pad_cols(router_w[l], LANES))
        rb = _pad_cols(router_b[l][None, :], LANES)
        wo = w_out[l].astype(BF16)
        if l < N_A:
            w = w_in_a[l]
            wg = jnp.transpose(w[:, :4 * MAIN_WIDTH].reshape(d, 4 * A_HEADS, hd), (1, 0, 2)).astype(BF16)
            wgh, wgl = _hi_lo(_pad_cols(w[:, 4 * MAIN_WIDTH:4 * MAIN_WIDTH + 2 * A_HEADS], LANES))
            wqm = w[:, 4 * MAIN_WIDTH + 2 * A_HEADS:].astype(BF16)
            p, qm, gates = proj_a(x, wg, wqm, wgh, wgl, tm=tm)
            gates = gates[:, :2 * A_HEADS] + b_gate_a[l][None, :]
            g_p = jnp.transpose(gates[:n_p].reshape(bp, seq, 2 * A_HEADS), (0, 2, 1))
            g_s = jnp.transpose(gates[n_p:].reshape(bs, dseq, 2 * A_HEADS), (0, 2, 1))
            zc = jnp.zeros((bp, A_HEADS, hd, hd), F32)
            znm = jnp.zeros((bp, A_HEADS, 8, 2 * LANES), F32)
            h_p, c_p, nm_p = mlstm(p, g_p, zc, znm, batch=bp, seq=seq, row_base=0,
                                   rows_per_step=4 * MLSTM_CHUNK, chunk=MLSTM_CHUNK)
            h_s, c_s, nm_s = mlstm(p, g_s, state_mlstm_C[l], pack_nm(state_mlstm_n[l], state_mlstm_m[l]),
                                   batch=bs, seq=dseq, row_base=n_p, rows_per_step=dseq, chunk=dseq)
            cs.append((c_p, c_s))
            ns.append((nm_p[:, :, 0, :hd], nm_s[:, :, 0, :hd]))
            ms.append((nm_p[:, :, 1, 0], nm_s[:, :, 1, 0]))
            hm_p = mem_attend(qm, 0, p_mem_k[l], p_mem_v[l], batch=bp, seq=seq, row_base=0, tq=512)
            hm_s = mem_attend(qm, 0, s_mem_k[l], s_mem_v[l], batch=bs, seq=dseq, row_base=n_p, tq=dseq)
            wo_main = wo[:MAIN_WIDTH].reshape(A_HEADS, hd, d)
            head_major = True
        else:
            if l == N_A:
                kv32, kv16 = matmul(x, w_kv_b.astype(BF16), tm=tm, tn=512, out_dtypes=(F32, BF16))
                k_new = kv32[:, :MAIN_WIDTH]
                v_new = kv32[:, MAIN_WIDTH:]
            (pq,) = matmul(x, w_in_b[l - N_A].astype(BF16), tm=tm, tn=512)
            h_p = sb_prompt(pq, kv16, batch=bp, seq=seq, tq=256, heads=4)
            q_s = pq[n_p:, :MAIN_WIDTH].reshape(bs, dseq, B_HEADS, B_HEAD_DIM) * (B_HEAD_DIM ** -0.5)
            qbd = jnp.einsum("bthd,hg->bhdgt", q_s, jnp.eye(B_HEADS, dtype=F32))
            qbd = qbd.reshape(bs, MAIN_WIDTH, B_HEADS * dseq).astype(BF16)
            h_s = sb_sample(qbd, k_new[n_p:].reshape(bs, dseq, MAIN_WIDTH),
                            v_new[n_p:].reshape(bs, dseq, MAIN_WIDTH), k_cache, v_cache,
                            tblk=1024, tsub=256).reshape(n_s, MAIN_WIDTH)
            hm_p = mem_attend(pq, MAIN_WIDTH // MEM_WIDTH, p_mem_k[l], p_mem_v[l],
                              batch=bp, seq=seq, row_base=0, tq=512)
            hm_s = mem_attend(pq, MAIN_WIDTH // MEM_WIDTH, s_mem_k[l], s_mem_v[l],
                              batch=bs, seq=dseq, row_base=n_p, tq=dseq)
            wo_main = wo[:MAIN_WIDTH]
            head_major = False
        x1, x1b, meta, metat, stats = mixer_tail(
            x, h_p, h_s, hm_p, hm_s, wo_main, wo[MAIN_WIDTH:], ln_g[l, 0][None, :], ln_b[l, 0][None, :],
            rwh, rwl, rb, head_major=head_major)
        x = moe_layer(x1, x1b, meta, metat, stats, ln_g[l, 1][None, :], ln_b[l, 1][None, :],
                      wgu_all, bgu_all, wd_all, bd_all, l)

    y_prompt = x[:n_p].reshape(bp, seq, d)
    y_sample = x[n_p:].reshape(bs, dseq, d)
    p_c = jnp.stack([c[0] for c in cs])
    s_c = jnp.stack([c[1] for c in cs])
    p_n = jnp.stack([n[0] for n in ns])
    s_n = jnp.stack([n[1] for n in ns])
    p_m = jnp.stack([m[0] for m in ms])
    s_m = jnp.stack([m[1] for m in ms])
    p_sb_k = k_new[:n_p].reshape(bp, seq, B_HEADS, B_HEAD_DIM)
    p_sb_v = v_new[:n_p].reshape(bp, seq, B_HEADS, B_HEAD_DIM)
    s_sb_k = k_new[n_p:].reshape(bs, dseq, B_HEADS, B_HEAD_DIM)
    s_sb_v = v_new[n_p:].reshape(bs, dseq, B_HEADS, B_HEAD_DIM)
    pmk = p_mem_k.reshape(DEPTH, bp, n_mem, MEM_HEADS, MEM_HEAD_DIM)
    pmv = p_mem_v.reshape(DEPTH, bp, n_mem, MEM_HEADS, MEM_HEAD_DIM)
    return (y_prompt, y_sample, p_c, p_n, p_m, p_sb_k, p_sb_v, pmk, pmv, s_c, s_n, s_m, s_sb_k, s_sb_v)
```

```python
import functools

import jax
import jax.numpy as jnp
from jax import lax
from jax.experimental import pallas as pl
from jax.experimental.pallas import tpu as pltpu

F32 = jnp.float32
BF16 = jnp.bfloat16
I32 = jnp.int32

D_MODEL = 1024
DEPTH = 4
N_A = DEPTH // 2
MEM_WIDTH = D_MODEL // 4
MAIN_WIDTH = D_MODEL - MEM_WIDTH
MEM_HEADS = 4
MEM_HEAD_DIM = MEM_WIDTH // MEM_HEADS
A_HEADS = 4
A_HEAD_DIM = MAIN_WIDTH // A_HEADS
B_HEADS = 12
B_HEAD_DIM = MAIN_WIDTH // B_HEADS
N_EXPERTS = 32
TOP_K = 4
D_FF = D_MODEL
SWIGLU_LIMIT = 7.0
SWIGLU_ALPHA = 1.702
DN_ALPHA = (2.0 * DEPTH) ** 0.25
LN_EPS = 1e-5
LOG2E = 1.4426950408889634

LANES = 128
PACK_ROWS = 16
TOK_TILE = 512
EXP_TILE = 512
MOE_CHUNK = 512
MLSTM_CHUNK = 128
VMEM_LIMIT = 56 * 1024 * 1024


def _cparams(*sem):
    return pltpu.CompilerParams(dimension_semantics=sem, vmem_limit_bytes=VMEM_LIMIT)


def _dot(a, b):
    return jnp.dot(a, b, preferred_element_type=F32)


def _dot_nt(a, b):
    return lax.dot_general(a, b, (((1,), (1,)), ((), ())), preferred_element_type=F32)


def _dot_tn(a, b):
    return lax.dot_general(a, b, (((0,), (0,)), ((), ())), preferred_element_type=F32)


def _split_bf16(x):
    hi = x.astype(BF16)
    lo = (x - hi.astype(F32)).astype(BF16)
    return hi, lo


def _softplus(x):
    return jnp.maximum(x, 0.0) + jnp.log(1.0 + jnp.exp(-jnp.abs(x)))


def _neg_abs(x):
    bits = lax.bitcast_convert_type(x, jnp.uint32) | jnp.uint32(0x80000000)
    return lax.bitcast_convert_type(bits, F32)


def _mm_kernel(x_ref, w_ref, *o_refs):
    y = _dot(x_ref[...].astype(BF16), w_ref[...])
    for o_ref in o_refs:
        o_ref[...] = y.astype(o_ref.dtype)


def matmul(x, w, *, tm, tn, out_dtypes=(F32,)):
    m, k = x.shape
    n = w.shape[1]
    outs = pl.pallas_call(
        _mm_kernel,
        grid=(m // tm, n // tn),
        in_specs=[pl.BlockSpec((tm, k), lambda i, j: (i, 0)),
                  pl.BlockSpec((k, tn), lambda i, j: (0, j))],
        out_specs=[pl.BlockSpec((tm, tn), lambda i, j: (i, j)) for _ in out_dtypes],
        out_shape=[jax.ShapeDtypeStruct((m, n), dt) for dt in out_dtypes],
        compiler_params=_cparams("parallel", "arbitrary"),
        name="matmul",
    )(x, w)
    return outs


def _proj_a_kernel(x_ref, wg_ref, wqm_ref, wgh_ref, wgl_ref, p_ref, qm_ref, gate_ref):
    x = x_ref[...]
    xh, xl = _split_bf16(x)
    for g in range(wg_ref.shape[0]):
        p_ref[g] = _dot(xh, wg_ref[g])

    @pl.when(pl.program_id(1) == 0)
    def _():
        qm_ref[...] = _dot(xh, wqm_ref[...])
        gate_ref[...] = _dot(xh, wgh_ref[...]) + _dot(xl, wgh_ref[...]) + _dot(xh, wgl_ref[...])


def proj_a(x, wg, wqm, wgh, wgl, *, tm):
    m, k = x.shape
    g, _, hd = wg.shape
    gs = A_HEADS
    return pl.pallas_call(
        _proj_a_kernel,
        grid=(m // tm, g // gs),
        in_specs=[pl.BlockSpec((tm, k), lambda i, j: (i, 0)),
                  pl.BlockSpec((gs, k, hd), lambda i, j: (j, 0, 0)),
                  pl.BlockSpec((k, MEM_WIDTH), lambda i, j: (0, 0)),
                  pl.BlockSpec((k, LANES), lambda i, j: (0, 0)),
                  pl.BlockSpec((k, LANES), lambda i, j: (0, 0))],
        out_specs=[pl.BlockSpec((gs, tm, hd), lambda i, j: (j, i, 0)),
                   pl.BlockSpec((tm, MEM_WIDTH), lambda i, j: (i, 0)),
                   pl.BlockSpec((tm, LANES), lambda i, j: (i, 0))],
        out_shape=[jax.ShapeDtypeStruct((g, m, hd), F32),
                   jax.ShapeDtypeStruct((m, MEM_WIDTH), F32),
                   jax.ShapeDtypeStruct((m, LANES), F32)],
        compiler_params=_cparams("parallel", "arbitrary"),
        name="proj_a",
    )(x, wg, wqm, wgh, wgl)


def _mlstm_kernel(q_ref, k_ref, v_ref, o_ref, g_ref, c0_ref, nm0_ref,
                  h_ref, cn_ref, nmn_ref, c_s, nm_s, *, chunk, n_chunks, heads, hd):
    step = pl.program_id(1)
    L = chunk

    @pl.when(step == 0)
    def _():
        c_s[...] = c0_ref[0]
        nm_s[...] = nm0_ref[0]

    row = lax.broadcasted_iota(I32, (L, L), 0)
    col = lax.broadcasted_iota(I32, (L, L), 1)
    eye = row == col
    tril = col <= row
    scale = hd ** -0.5

    for ch in range(n_chunks):
        rows = slice(ch * L, (ch + 1) * L)
        for h in range(heads):
            q = q_ref[h, rows, :]
            k = k_ref[h, rows, :] * scale
            v = v_ref[h, rows, :]
            li = g_ref[0, h:h + 1, rows]
            lf = -_softplus(-g_ref[0, heads + h:heads + h + 1, rows])
            lf_b = jnp.broadcast_to(lf, (L, L))
            b_col = jnp.sum(jnp.where(tril, lf_b, 0.0), axis=1, keepdims=True)
            lf_col = jnp.sum(jnp.where(eye, lf_b, 0.0), axis=1, keepdims=True)
            b_row = jnp.sum(jnp.where(row <= col, jnp.broadcast_to(lf_col, (L, L)), 0.0),
                            axis=0, keepdims=True)
            m_prev = nm_s[h, 1:2, 0:1]
            n_row = nm_s[h, 0:1, 0:hd]
            c_old = c_s[h]
            d = jnp.where(tril, b_col - b_row + li, -jnp.inf)
            inter = b_col + m_prev
            m_t = jnp.maximum(inter, jnp.max(d, axis=1, keepdims=True))
            g = jnp.exp(inter - m_t)
            qb = q.astype(BF16)
            kb = k.astype(BF16)
            vb = v.astype(BF16)
            s = _dot_nt(qb, kb) * jnp.exp(d - m_t)
            num = g * _dot(qb, c_old.astype(BF16)) + _dot(s.astype(BF16), vb)
            den = g * jnp.sum(q * n_row, axis=1, keepdims=True) + jnp.sum(s, axis=1, keepdims=True)
            hh = num / jnp.maximum(jnp.abs(den), jnp.exp(-m_t))
            h_ref[h, rows, :] = hh * jax.nn.sigmoid(o_ref[h, rows, :])

            b_last = b_col[L - 1:L, :]
            dec = b_last - b_row + li
            m_new = jnp.maximum(b_last + m_prev, jnp.max(dec, axis=1, keepdims=True))
            g_c = jnp.exp(b_last + m_prev - m_new)
            w_row = jnp.exp(dec - m_new)
            w_col = jnp.sum(jnp.where(eye, jnp.broadcast_to(w_row, (L, L)), 0.0), axis=1, keepdims=True)
            kw = k * w_col
            c_s[h] = g_c * c_old + _dot_tn(kw.astype(BF16), vb)
            nm_s[h, 0:1, 0:hd] = g_c * n_row + jnp.sum(kw, axis=0, keepdims=True)
            nm_s[h, 1:2, :] = jnp.broadcast_to(m_new, (1, nm_s.shape[2]))

    @pl.when(step == pl.num_programs(1) - 1)
    def _():
        cn_ref[0] = c_s[...]
        nmn_ref[0] = nm_s[...]


def mlstm(p, gates, c0, nm0, *, batch, seq, row_base, rows_per_step, chunk):
    hd = p.shape[-1]
    heads = A_HEADS
    steps = seq // rows_per_step
    base = row_base // rows_per_step
    nmw = nm0.shape[-1]

    def tok_map(grp):
        return lambda b, c: (grp, base + b * steps + c, 0)

    kern = functools.partial(_mlstm_kernel, chunk=chunk, n_chunks=rows_per_step // chunk,
                             heads=heads, hd=hd)
    return pl.pallas_call(
        kern,
        grid=(batch, steps),
        in_specs=[pl.BlockSpec((heads, rows_per_step, hd), tok_map(0)),
                  pl.BlockSpec((heads, rows_per_step, hd), tok_map(1)),
                  pl.BlockSpec((heads, rows_per_step, hd), tok_map(2)),
                  pl.BlockSpec((heads, rows_per_step, hd), tok_map(3)),
                  pl.BlockSpec((1, 2 * heads, rows_per_step), lambda b, c: (b, 0, c)),
                  pl.BlockSpec((1, heads, hd, hd), lambda b, c: (b, 0, 0, 0)),
                  pl.BlockSpec((1, heads, 8, nmw), lambda b, c: (b, 0, 0, 0))],
        out_specs=[pl.BlockSpec((heads, rows_per_step, hd), lambda b, c: (0, b * steps + c, 0)),
                   pl.BlockSpec((1, heads, hd, hd), lambda b, c: (b, 0, 0, 0)),
                   pl.BlockSpec((1, heads, 8, nmw), lambda b, c: (b, 0, 0, 0))],
        out_shape=[jax.ShapeDtypeStruct((heads, batch * seq, hd), F32),
                   jax.ShapeDtypeStruct((batch, heads, hd, hd), F32),
                   jax.ShapeDtypeStruct((batch, heads, 8, nmw), F32)],
        scratch_shapes=[pltpu.VMEM((heads, hd, hd), F32), pltpu.VMEM((heads, 8, nmw), F32)],
        compiler_params=_cparams("parallel", "arbitrary"),
        name="mlstm",
    )(p, p, p, p, gates, c0, nm0)


def _mem_kernel(q_ref, mk_ref, mv_ref, o_ref):
    q = q_ref[...]
    mk = mk_ref[0].astype(BF16)
    mv = mv_ref[0].astype(BF16)
    lane = lax.broadcasted_iota(I32, q.shape, 1)
    out = jnp.zeros(q.shape, F32)
    for h in range(MEM_HEADS):
        sel = (lane >= h * MEM_HEAD_DIM) & (lane < (h + 1) * MEM_HEAD_DIM)
        qh = jnp.where(sel, q, 0.0).astype(BF16)
        s = _dot_nt(qh, mk) * (MEM_HEAD_DIM ** -0.5)
        s = s - jnp.max(s, axis=1, keepdims=True)
        e = jnp.exp(s)
        p = e / jnp.sum(e, axis=1, keepdims=True)
        out = jnp.where(sel, _dot(p.astype(BF16), mv), out)
    o_ref[...] = out


def mem_attend(qsrc, col_block, mk, mv, *, batch, seq, row_base, tq):
    steps = seq // tq
    base = row_base // tq
    return pl.pallas_call(
        _mem_kernel,
        grid=(batch, steps),
        in_specs=[pl.BlockSpec((tq, MEM_WIDTH), lambda b, i: (base + b * steps + i, col_block)),
                  pl.BlockSpec((1,) + mk.shape[1:], lambda b, i: (b, 0, 0)),
                  pl.BlockSpec((1,) + mv.shape[1:], lambda b, i: (b, 0, 0))],
        out_specs=pl.BlockSpec((tq, MEM_WIDTH), lambda b, i: (b * steps + i, 0)),
        out_shape=jax.ShapeDtypeStruct((batch * seq, MEM_WIDTH), F32),
        compiler_params=_cparams("parallel", "arbitrary"),
        name="mem_attend",
    )(qsrc, mk, mv)


def _sbp_kernel(q_ref, k_ref, v_ref, o_ref, w_s, tot_s, car_s, acc_s, *, tq, heads):
    qi = pl.program_id(2)
    qn = q_ref[...] * (-(B_HEAD_DIM ** -0.5) * LOG2E)
    lane = lax.broadcasted_iota(I32, (tq, LANES), 1)
    first = lane < B_HEAD_DIM
    qs = []
    for p in range(heads // 2):
        qp = qn[:, p * LANES:(p + 1) * LANES]
        qs += [jnp.where(first, qp, 0.0).astype(BF16), jnp.where(first, 0.0, qp).astype(BF16)]
    r = lax.broadcasted_iota(I32, (tq, tq), 0)
    c = lax.broadcasted_iota(I32, (tq, tq), 1)
    tri = jnp.where(r >= c, 1.0, 0.0).astype(BF16)
    causal = c < r

    def cols(h):
        return slice((h // 2) * LANES, (h // 2 + 1) * LANES)

    def x_logits(j):
        start = pl.multiple_of(j * tq, tq)
        return [_dot_nt(qs[h], k_ref[pl.ds(start, tq), cols(h)]) for h in range(heads)]

    def x_finish(zns, masked):
        cins = []
        for zn in zns:
            l1m = jnp.minimum(zn, 0.0) - jnp.log2(1.0 + jnp.exp2(_neg_abs(zn)))
            if masked:
                l1m = jnp.where(causal, l1m, 0.0)
            cins.append(_dot(l1m.astype(BF16), tri))
        for h in range(heads):
            w = cins[h] - zns[h]
            if masked:
                w = jnp.where(causal, w, -jnp.inf)
            w_s[h] = w
            tot_s[h] = cins[h][:, 0:1]

    def stage_y(j):
        start = pl.multiple_of(j * tq, tq)
        for h in range(heads):
            car = car_s[h]
            a = jnp.exp2(w_s[h] + car)
            acc_s[h] += _dot(a.astype(BF16), v_ref[pl.ds(start, tq), cols(h)])
            car_s[h] = car + tot_s[h]

    acc_s[...] = jnp.zeros(acc_s.shape, F32)
    car_s[...] = jnp.zeros(car_s.shape, F32)
    x_finish(x_logits(qi), True)

    def body(it, _):
        j = qi - 1 - it
        zns = x_logits(j)
        stage_y(j + 1)
        x_finish(zns, False)
        return 0

    lax.fori_loop(0, qi, body, 0)
    stage_y(0)
    for p in range(heads // 2):
        o_ref[:, p * LANES:(p + 1) * LANES] = jnp.where(first, acc_s[2 * p], acc_s[2 * p + 1])


def sb_prompt(qsrc, kv, *, batch, seq, tq, heads):
    nq = seq // tq
    width = heads * B_HEAD_DIM
    groups = MAIN_WIDTH // width
    return pl.pallas_call(
        functools.partial(_sbp_kernel, tq=tq, heads=heads),
        grid=(batch, groups, nq),
        in_specs=[pl.BlockSpec((tq, width), lambda b, p, i: (b * nq + i, p)),
                  pl.BlockSpec((seq, width), lambda b, p, i: (b, p)),
                  pl.BlockSpec((seq, width), lambda b, p, i: (b, groups + p))],
        out_specs=pl.BlockSpec((tq, width), lambda b, p, i: (b * nq + i, p)),
        out_shape=jax.ShapeDtypeStruct((batch * seq, MAIN_WIDTH), F32),
        scratch_shapes=[pltpu.VMEM((heads, tq, tq), F32), pltpu.VMEM((heads, tq, 1), F32),
                        pltpu.VMEM((heads, tq, 1), F32), pltpu.VMEM((heads, tq, LANES), F32)],
        compiler_params=_cparams("parallel", "parallel", "arbitrary"),
        name="sb_prompt",
    )(qsrc, kv, kv)


def _sbs_kernel(qbd_ref, kn_ref, vn_ref, kc_ref, vc_ref, o_ref, acc_s, car_s, *, tsub, n_sub, tnew):
    j = pl.program_id(1)
    qbd = qbd_ref[0]
    width = qbd.shape[1]

    def block(k, v, tk, mask):
        r = lax.broadcasted_iota(I32, (tk, tk), 0)
        c = lax.broadcasted_iota(I32, (tk, tk), 1)
        tri = jnp.where(c >= r, 1.0, 0.0).astype(BF16)
        z = _dot(k.astype(BF16), qbd)
        l1m = -_softplus(z)
        if mask is not None:
            l1m = jnp.where(mask, l1m, 0.0)
        hi, lo = _split_bf16(l1m)
        cin = _dot(tri, hi) + _dot(tri, lo)
        a = jnp.exp(z + car_s[...] + cin)
        if mask is not None:
            a = jnp.where(mask, a, 0.0)
        car_s[...] += cin[0:1, :]
        acc_s[...] += _dot_tn(a.astype(BF16), v.astype(BF16))

    @pl.when(j == 0)
    def _():
        acc_s[...] = jnp.zeros(acc_s.shape, F32)
        car_s[...] = jnp.zeros(car_s.shape, F32)
        s_idx = lax.broadcasted_iota(I32, (tnew, width), 0)
        t_idx = lax.broadcasted_iota(I32, (tnew, width), 1) % tnew
        block(kn_ref[0], vn_ref[0], tnew, s_idx < t_idx)

    for sub in range(n_sub - 1, -1, -1):
        rows = slice(sub * tsub, (sub + 1) * tsub)
        block(kc_ref[0, rows, :], vc_ref[0, rows, :], tsub, None)

    @pl.when(j == pl.num_programs(1) - 1)
    def _():
        heads = width // tnew
        acc = acc_s[...].reshape(heads, tnew, acc_s.shape[1])
        hrow = lax.broadcasted_iota(I32, acc.shape, 0)
        hcol = lax.broadcasted_iota(I32, acc.shape, 2) // B_HEAD_DIM
        o_ref[0] = jnp.sum(jnp.where(hrow == hcol, acc, 0.0), axis=0)


def sb_sample(qbd, k_new, v_new, k_cache, v_cache, *, tblk, tsub):
    batch, past, width = k_cache.shape
    tnew = k_new.shape[1]
    nblk = past // tblk
    return pl.pallas_call(
        functools.partial(_sbs_kernel, tsub=tsub, n_sub=tblk // tsub, tnew=tnew),
        grid=(batch, nblk),
        in_specs=[pl.BlockSpec((1,) + qbd.shape[1:], lambda b, j: (b, 0, 0)),
                  pl.BlockSpec((1, tnew, width), lambda b, j: (b, 0, 0)),
                  pl.BlockSpec((1, tnew, width), lambda b, j: (b, 0, 0)),
                  pl.BlockSpec((1, tblk, width), lambda b, j: (b, nblk - 1 - j, 0)),
                  pl.BlockSpec((1, tblk, width), lambda b, j: (b, nblk - 1 - j, 0))],
        out_specs=pl.BlockSpec((1, tnew, width), lambda b, j: (b, 0, 0)),
        out_shape=jax.ShapeDtypeStruct((batch, tnew, width), F32),
        scratch_shapes=[pltpu.VMEM((qbd.shape[2], width), F32), pltpu.VMEM((1, qbd.shape[2]), F32)],
        compiler_params=_cparams("parallel", "arbitrary"),
        name="sb_sample",
    )(qbd, k_new, v_new, k_cache, v_cache)


def _layer_norm(x, g, b):
    mu = jnp.mean(x, axis=1, keepdims=True)
    xc = x - mu
    var = jnp.mean(xc * xc, axis=1, keepdims=True)
    return xc * lax.rsqrt(var + LN_EPS) * g + b


def _tail_kernel(*refs, n_main, head_major, prompt_tiles):
    x_ref = refs[0]
    mainp_refs = refs[1:1 + n_main]
    mains_refs = refs[1 + n_main:1 + 2 * n_main]
    rest = refs[1 + 2 * n_main:]
    hmp_ref, hms_ref, wo_ref, wom_ref, g_ref, b_ref, rwh_ref, rwl_ref, rb_ref = rest[:9]
    x1_ref, x1b_ref, meta_ref, metat_ref, stat_ref = rest[9:]
    t = x_ref.shape[0]
    is_sample = pl.program_id(0) >= prompt_tiles

    def pick(p_ref, s_ref):
        if head_major:
            return jnp.where(is_sample, s_ref[0], p_ref[0]).astype(BF16)
        return jnp.where(is_sample, s_ref[...], p_ref[...]).astype(BF16)

    a = _dot(jnp.where(is_sample, hms_ref[...], hmp_ref[...]).astype(BF16), wom_ref[...])
    if head_major:
        for h in range(n_main):
            a += _dot(pick(mainp_refs[h], mains_refs[h]), wo_ref[h])
    else:
        a += _dot(pick(mainp_refs[0], mains_refs[0]), wo_ref[...])
    x1 = _layer_norm(DN_ALPHA * x_ref[...] + a, g_ref[...], b_ref[...])
    x1_ref[...] = x1
    x1b_ref[...] = x1.astype(BF16)

    xh, xl = _split_bf16(x1)
    logits = _dot(xh, rwh_ref[...]) + _dot(xl, rwh_ref[...]) + _dot(xh, rwl_ref[...]) + rb_ref[...]
    lane = lax.broadcasted_iota(I32, (t, LANES), 1)
    logits = jnp.where(lane < N_EXPERTS, logits, -jnp.inf)
    lane_f = lane.astype(F32)
    vals, sels = [], []
    for _ in range(TOP_K):
        mx = jnp.max(logits, axis=1, keepdims=True)
        idx = jnp.min(jnp.where(logits == mx, lane_f, float(LANES)), axis=1, keepdims=True)
        sel = lane_f == idx
        vals.append(mx)
        sels.append(sel)
        logits = jnp.where(sel, -jnp.inf, logits)
    exps = [jnp.exp(v - vals[0]) for v in vals]
    tot = exps[0] + exps[1] + exps[2] + exps[3]
    wts = [e / tot for e in exps]

    onehot = jnp.zeros((t, LANES), F32)
    for sel in sels:
        onehot = jnp.where(sel, 1.0, onehot)
    cnt = jnp.sum(onehot, axis=0, keepdims=True)
    pc = jnp.floor((cnt + (PACK_ROWS - 1)) * (1.0 / PACK_ROWS))
    er = lax.broadcasted_iota(I32, (LANES, LANES), 0)
    ec = lax.broadcasted_iota(I32, (LANES, LANES), 1)
    before = jnp.where(er < ec, 1.0, 0.0).astype(BF16)
    off = _dot(jnp.broadcast_to(pc, (8, LANES)).astype(BF16), before)[0:1, :]
    tr = lax.broadcasted_iota(I32, (t, t), 0)
    tc = lax.broadcasted_iota(I32, (t, t), 1)
    earlier = jnp.where(tc < tr, 1.0, 0.0).astype(BF16)
    rank = _dot(earlier, onehot.astype(BF16))
    posfull = off * float(PACK_ROWS) + rank
    meta = jnp.zeros((t, LANES), F32)
    for k in range(TOP_K):
        pos_k = jnp.sum(jnp.where(sels[k], posfull, 0.0), axis=1, keepdims=True)
        meta = jnp.where(lane == k, pos_k, meta)
        meta = jnp.where(lane == TOP_K + k, wts[k], meta)
    meta_ref[...] = meta
    metat_ref[0] = meta.T[0:8, :]
    srow = lax.broadcasted_iota(I32, (8, LANES), 0)
    stat_ref[0] = jnp.where(srow == 0, pc, jnp.where(srow == 1, off, 0.0))


def mixer_tail(x, main_p, main_s, hm_p, hm_s, wo_main, wo_mem, ln_g, ln_b, rwh, rwl, rb, *, head_major):
    n_tok, d = x.shape
    t = TOK_TILE
    nt = n_tok // t
    assert hm_s.shape[0] == t and hm_p.shape[0] == (nt - 1) * t
    last_p = nt - 2
    if head_major:
        n_main, _, hd = main_p.shape
        mainp_specs = [pl.BlockSpec((1, t, hd), functools.partial(lambda h, i: (h, jnp.minimum(i, last_p), 0), h))
                       for h in range(n_main)]
        mains_specs = [pl.BlockSpec((1, t, hd), functools.partial(lambda h, i: (h, 0, 0), h))
                       for h in range(n_main)]
        wo_spec = pl.BlockSpec(wo_main.shape, lambda i: (0, 0, 0))
    else:
        n_main = 1
        mainp_specs = [pl.BlockSpec((t, MAIN_WIDTH), lambda i: (jnp.minimum(i, last_p), 0))]
        mains_specs = [pl.BlockSpec((t, MAIN_WIDTH), lambda i: (0, 0))]
        wo_spec = pl.BlockSpec(wo_main.shape, lambda i: (0, 0))
    const2 = lambda i: (0, 0)
    return pl.pallas_call(
        functools.partial(_tail_kernel, n_main=n_main, head_major=head_major, prompt_tiles=nt - 1),
        grid=(nt,),
        in_specs=[pl.BlockSpec((t, d), lambda i: (i, 0))] + mainp_specs + mains_specs + [
            pl.BlockSpec((t, MEM_WIDTH), lambda i: (jnp.minimum(i, last_p), 0)),
            pl.BlockSpec((t, MEM_WIDTH), const2),
            wo_spec,
            pl.BlockSpec(wo_mem.shape, const2),
            pl.BlockSpec((1, d), const2), pl.BlockSpec((1, d), const2),
            pl.BlockSpec((d, LANES), const2), pl.BlockSpec((d, LANES), const2),
            pl.BlockSpec((1, LANES), const2)],
        out_specs=[pl.BlockSpec((t, d), lambda i: (i, 0)),
                   pl.BlockSpec((t, d), lambda i: (i, 0)),
                   pl.BlockSpec((t, LANES), lambda i: (i, 0)),
                   pl.BlockSpec((1, 8, t), lambda i: (i, 0, 0)),
                   pl.BlockSpec((1, 8, LANES), lambda i: (i, 0, 0))],
        out_shape=[jax.ShapeDtypeStruct((n_tok, d), F32),
                   jax.ShapeDtypeStruct((n_tok, d), BF16),
                   jax.ShapeDtypeStruct((n_tok, LANES), F32),
                   jax.ShapeDtypeStruct((nt, 8, t), F32),
                   jax.ShapeDtypeStruct((nt, 8, LANES), F32)],
        compiler_params=_cparams("parallel"),
        name="mixer_tail",
    )(x, *([main_p] * n_main), *([main_s] * n_main), hm_p, hm_s, wo_main, wo_mem, ln_g, ln_b, rwh, rwl, rb)


def _tile_rows():
    need = TOK_TILE * TOP_K + N_EXPERTS * (PACK_ROWS - 1)
    return -(-need // 256) * 256


def _piece_copies(dst_ref, np_ref, tile, make_copy, action):
    pieces = _tile_rows() // PACK_ROWS

    def per_piece(p, _):
        s = pl.multiple_of(p * PACK_ROWS, PACK_ROWS)
        g = pl.multiple_of(dst_ref[tile * pieces + p] * PACK_ROWS, PACK_ROWS)
        action(make_copy(s, g))
        return 0

    lax.fori_loop(0, np_ref[tile], per_piece, 0)


def _zero_fill(ts_ref, tn_ref, nu_ref, xg_ref, zero_s, sem, action):
    def per_expert(e, _):
        def per_piece(p, _):
            g = pl.multiple_of((ts_ref[e] + p) * PACK_ROWS, PACK_ROWS)
            action(pltpu.make_async_copy(zero_s.at[pl.ds(0, PACK_ROWS)], xg_ref.at[pl.ds(g, PACK_ROWS)], sem))
            return 0

        lax.fori_loop(0, tn_ref[e], per_piece, 0)
        return 0

    lax.fori_loop(0, N_EXPERTS, per_expert, 0)

    def per_tile(i, _):
        g = pl.multiple_of(i * EXP_TILE, EXP_TILE)
        action(pltpu.make_async_copy(zero_s, xg_ref.at[pl.ds(g, EXP_TILE)], sem))
        return 0

    lax.fori_loop(nu_ref[0], xg_ref.shape[0] // EXP_TILE, per_tile, 0)


def _dispatch_kernel(dst_ref, np_ref, ts_ref, tn_ref, nu_ref, metat_ref, x_ref, xg_ref,
                     xs_s, zero_s, sem, zsem):
    tile = pl.program_id(0)
    rt, t = xs_s.shape[1], x_ref.shape[0]
    buf = tile % 2

    @pl.when(tile == 0)
    def _():
        zero_s[...] = jnp.zeros(zero_s.shape, zero_s.dtype)
        _zero_fill(ts_ref, tn_ref, nu_ref, xg_ref, zero_s, zsem, lambda cp: cp.start())
        _zero_fill(ts_ref, tn_ref, nu_ref, xg_ref, zero_s, zsem, lambda cp: cp.wait())

    posr = metat_ref[0]
    x = x_ref[...]
    for c in range(rt // MOE_CHUNK):
        slot = (lax.broadcasted_iota(I32, (MOE_CHUNK, t), 0) + c * MOE_CHUNK).astype(F32)
        hit = slot == posr[0:1, :]
        for k in range(1, TOP_K):
            hit = hit | (slot == posr[k:k + 1, :])
        rows = slice(c * MOE_CHUNK, (c + 1) * MOE_CHUNK)
        xs_s[buf, rows, :] = _dot(jnp.where(hit, 1.0, 0.0).astype(BF16), x).astype(BF16)

    def copies_of(b):
        def make_copy(s, g):
            return pltpu.make_async_copy(xs_s.at[b, pl.ds(s, PACK_ROWS)], xg_ref.at[pl.ds(g, PACK_ROWS)],
                                         sem.at[b])
        return make_copy

    @pl.when(tile > 0)
    def _():
        _piece_copies(dst_ref, np_ref, tile - 1, copies_of(1 - buf), lambda cp: cp.wait())

    _piece_copies(dst_ref, np_ref, tile, copies_of(buf), lambda cp: cp.start())

    @pl.when(tile == pl.num_programs(0) - 1)
    def _():
        _piece_copies(dst_ref, np_ref, tile, copies_of(buf), lambda cp: cp.wait())


def moe_dispatch(dstp, npieces, tail_start, tail_n, nused, metat, x1b, n_rows):
    n_tok, d = x1b.shape
    t = TOK_TILE
    rt = _tile_rows()
    return pl.pallas_call(
        _dispatch_kernel,
        grid_spec=pltpu.PrefetchScalarGridSpec(
            num_scalar_prefetch=5,
            grid=(n_tok // t,),
            in_specs=[pl.BlockSpec((1, 8, t), lambda i, *_: (i, 0, 0)),
                      pl.BlockSpec((t, d), lambda i, *_: (i, 0))],
            out_specs=pl.BlockSpec(memory_space=pl.ANY),
            scratch_shapes=[pltpu.VMEM((2, rt, d), BF16), pltpu.VMEM((EXP_TILE, d), BF16),
                            pltpu.SemaphoreType.DMA((2,)), pltpu.SemaphoreType.DMA(())]),
        out_shape=jax.ShapeDtypeStruct((n_rows, d), BF16),
        compiler_params=_cparams("arbitrary"),
        name="moe_dispatch",
    )(dstp, npieces, tail_start, tail_n, nused, metat, x1b)


def _expert_kernel(te_ref, nu_ref, x_ref, wgu_ref, bgu_ref, wd_ref, bd_ref, y_ref, wgu_s, wd_s):
    i = pl.program_id(0)
    prev = te_ref[jnp.maximum(i - 1, 0)]

    @pl.when((i == 0) | (te_ref[i] != prev))
    def _():
        wgu_s[...] = wgu_ref[0].astype(BF16)
        wd_s[...] = wd_ref[0].astype(BF16)

    @pl.when(i < nu_ref[0])
    def _():
        x = x_ref[...]
        f = wd_s.shape[0]
        half = f // 2
        y = jnp.broadcast_to(bd_ref[0], y_ref.shape)
        for c in range(2):
            cols = slice(c * half, (c + 1) * half)
            ucols = slice(f + c * half, f + (c + 1) * half)
            gate = jnp.minimum(_dot(x, wgu_s[:, cols]) + bgu_ref[0, :, cols], SWIGLU_LIMIT)
            up = jnp.clip(_dot(x, wgu_s[:, ucols]) + bgu_ref[0, :, ucols], -SWIGLU_LIMIT, SWIGLU_LIMIT)
            act = (up + 1.0) * gate * jax.nn.sigmoid(SWIGLU_ALPHA * gate)
            y = y + _dot(act.astype(BF16), wd_s[cols, :])
        y_ref[...] = y.astype(y_ref.dtype)

    @pl.when(i >= nu_ref[0])
    def _():
        y_ref[...] = jnp.zeros(y_ref.shape, y_ref.dtype)


def moe_experts(te, nused, xg, wgu, bgu, wd, bd):
    n_rows, d = xg.shape
    f2 = wgu.shape[-1]
    tm = EXP_TILE

    def row_map(i, te_ref, nu_ref):
        return (jnp.minimum(i, nu_ref[0] - 1), 0)

    def w_map(i, te_ref, nu_ref):
        return (te_ref[i], 0, 0)

    return pl.pallas_call(
        _expert_kernel,
        grid_spec=pltpu.PrefetchScalarGridSpec(
            num_scalar_prefetch=2,
            grid=(n_rows // tm,),
            in_specs=[pl.BlockSpec((tm, d), row_map),
                      pl.BlockSpec((1, d, f2), w_map),
                      pl.BlockSpec((1, 1, f2), w_map),
                      pl.BlockSpec((1, f2 // 2, d), w_map),
                      pl.BlockSpec((1, 1, d), w_map)],
            out_specs=pl.BlockSpec((tm, d), lambda i, *_: (i, 0)),
            scratch_shapes=[pltpu.VMEM((d, f2), BF16), pltpu.VMEM((f2 // 2, d), BF16)]),
        out_shape=jax.ShapeDtypeStruct((n_rows, d), BF16),
        compiler_params=_cparams("arbitrary"),
        name="moe_experts",
    )(te, nused, xg, wgu, bgu, wd, bd)


def _combine_kernel(dst_ref, np_ref, meta_ref, x1_ref, g_ref, b_ref, yg_ref, x2_ref, ys_s, sem):
    tile = pl.program_id(0)
    rt, t = ys_s.shape[1], x1_ref.shape[0]
    buf = tile % 2

    def copies_of(b):
        def make_copy(s, g):
            return pltpu.make_async_copy(yg_ref.at[pl.ds(g, PACK_ROWS)], ys_s.at[b, pl.ds(s, PACK_ROWS)],
                                         sem.at[b])
        return make_copy

    @pl.when(tile == 0)
    def _():
        ys_s[...] = jnp.zeros(ys_s.shape, ys_s.dtype)
        _piece_copies(dst_ref, np_ref, tile, copies_of(buf), lambda cp: cp.start())

    @pl.when(tile + 1 < pl.num_programs(0))
    def _():
        _piece_copies(dst_ref, np_ref, tile + 1, copies_of(1 - buf), lambda cp: cp.start())

    _piece_copies(dst_ref, np_ref, tile, copies_of(buf), lambda cp: cp.wait())
    meta = meta_ref[...]
    f = jnp.zeros(x1_ref.shape, F32)
    for c in range(rt // MOE_CHUNK):
        slot = (lax.broadcasted_iota(I32, (t, MOE_CHUNK), 1) + c * MOE_CHUNK).astype(F32)
        wmat = jnp.zeros((t, MOE_CHUNK), F32)
        for k in range(TOP_K):
            wmat = jnp.where(slot == meta[:, k:k + 1], meta[:, TOP_K + k:TOP_K + k + 1], wmat)
        f += _dot(wmat.astype(BF16), ys_s[buf, c * MOE_CHUNK:(c + 1) * MOE_CHUNK, :])
    x2_ref[...] = _layer_norm(DN_ALPHA * x1_ref[...] + f, g_ref[...], b_ref[...])


def moe_combine(dstp, npieces, meta, x1, ln_g, ln_b, yg):
    n_tok, d = x1.shape
    t = TOK_TILE
    rt = _tile_rows()
    return pl.pallas_call(
        _combine_kernel,
        grid_spec=pltpu.PrefetchScalarGridSpec(
            num_scalar_prefetch=2,
            grid=(n_tok // t,),
            in_specs=[pl.BlockSpec((t, LANES), lambda i, *_: (i, 0)),
                      pl.BlockSpec((t, d), lambda i, *_: (i, 0)),
                      pl.BlockSpec((1, d), lambda i, *_: (0, 0)),
                      pl.BlockSpec((1, d), lambda i, *_: (0, 0)),
                      pl.BlockSpec(memory_space=pl.ANY)],
            out_specs=pl.BlockSpec((t, d), lambda i, *_: (i, 0)),
            scratch_shapes=[pltpu.VMEM((2, rt, d), BF16), pltpu.SemaphoreType.DMA((2,))]),
        out_shape=jax.ShapeDtypeStruct((n_tok, d), F32),
        compiler_params=_cparams("arbitrary"),
        name="moe_combine",
    )(dstp, npieces, meta, x1, ln_g, ln_b, yg)


def _route_tables(stats, n_rows):
    pc = stats[:, 0, :N_EXPERTS].astype(I32)
    off = stats[:, 1, :N_EXPERTS].astype(I32)
    pieces_e = jnp.sum(pc, axis=0)
    per_tile = EXP_TILE // PACK_ROWS
    tiles_e = (pieces_e + per_tile - 1) // per_tile
    ends = jnp.cumsum(tiles_e)
    base_e = (ends - tiles_e) * per_tile
    gd = base_e[None, :] + jnp.cumsum(pc, axis=0) - pc
    tail_start = base_e + pieces_e
    tail_n = tiles_e * per_tile - pieces_e
    nused = ends[-1]
    tile_ids = jnp.minimum(jnp.arange(n_rows // EXP_TILE, dtype=I32), nused - 1)
    te = jnp.sum((ends[None, :] <= tile_ids[:, None]).astype(I32), axis=1)
    seg_end = off + pc
    pidx = jnp.arange(_tile_rows() // PACK_ROWS, dtype=I32)
    e_of_p = jnp.sum((seg_end[:, None, :] <= pidx[None, :, None]).astype(I32), axis=2)
    sel = e_of_p[:, :, None] == jnp.arange(N_EXPERTS, dtype=I32)[None, None, :]
    dstp = jnp.sum(jnp.where(sel, (gd - off)[:, None, :], 0), axis=2) + pidx[None, :]
    npieces = seg_end[:, -1]
    return (dstp.reshape(-1).astype(I32), npieces.astype(I32), tail_start.astype(I32),
            tail_n.astype(I32), te, nused.reshape(1).astype(I32))


def moe_layer(x1, x1b, meta, metat, stats, ln_g, ln_b, wgu, bgu, wd, bd, layer):
    n_tok = x1.shape[0]
    nt = n_tok // TOK_TILE
    worst = n_tok * TOP_K + nt * N_EXPERTS * (PACK_ROWS - 1) + N_EXPERTS * (EXP_TILE - 1)
    n_rows = -(-worst // EXP_TILE) * EXP_TILE
    dstp, npieces, tail_start, tail_n, te, nused = _route_tables(stats, n_rows)
    xg = moe_dispatch(dstp, npieces, tail_start, tail_n, nused, metat, x1b, n_rows)
    yg = moe_experts(te + layer * N_EXPERTS, nused, xg, wgu, bgu, wd, bd)
    return moe_combine(dstp, npieces, meta, x1, ln_g, ln_b, yg)


def _pad_cols(w, width):
    return jnp.pad(w, ((0, 0), (0, width - w.shape[1])))


def _hi_lo(w):
    hi = w.astype(BF16)
    return hi, (w - hi.astype(F32)).astype(BF16)


def kernel(x_prompt, x_sample, state_mlstm_C, state_mlstm_n, state_mlstm_m, cache_sb_k, cache_sb_v,
           cache_mem_k, cache_mem_v, mem_prompt, w_in_a, b_gate_a, w_in_b, w_kv_b, w_mem_kv, w_out,
           ln_g, ln_b, router_w, router_b, w_gate_up, b_gate_up, w_down, b_down):
    bp, seq, d = x_prompt.shape
    bs, dseq, _ = x_sample.shape
    n_mem = mem_prompt.shape[1]
    past = cache_sb_k.shape[1]
    n_p = bp * seq
    n_s = bs * dseq
    hd = A_HEAD_DIM
    x = jnp.concatenate([x_prompt.reshape(n_p, d), x_sample.reshape(n_s, d)], axis=0)
    n_tok = n_p + n_s
    tm = n_tok // 26 if n_tok % 26 == 0 and (n_tok // 26) % 8 == 0 else TOK_TILE

    wm = jnp.transpose(w_mem_kv, (1, 0, 2)).reshape(d, DEPTH * 2 * MEM_WIDTH).astype(BF16)
    (mkv,) = matmul(mem_prompt.reshape(bp * n_mem, d), wm, tm=512, tn=512)
    mkv = mkv.reshape(bp, n_mem, DEPTH, 2, MEM_WIDTH)
    p_mem_k = jnp.transpose(mkv[:, :, :, 0], (2, 0, 1, 3))
    p_mem_v = jnp.transpose(mkv[:, :, :, 1], (2, 0, 1, 3))
    s_mem_k = cache_mem_k.reshape(DEPTH, bs, n_mem, MEM_WIDTH)
    s_mem_v = cache_mem_v.reshape(DEPTH, bs, n_mem, MEM_WIDTH)

    def pack_nm(n, m):
        b = n.shape[0]
        out = jnp.zeros((b, A_HEADS, 8, 2 * LANES), F32)
        out = out.at[:, :, 0, :hd].set(n)
        return out.at[:, :, 1, :].set(jnp.broadcast_to(m[:, :, None], (b, A_HEADS, 2 * LANES)))

    k_cache = cache_sb_k.reshape(bs, past, MAIN_WIDTH)
    v_cache = cache_sb_v.reshape(bs, past, MAIN_WIDTH)
    n_exp = DEPTH * N_EXPERTS
    wgu_all = w_gate_up.reshape(n_exp, d, 2 * D_FF)
    bgu_all = b_gate_up.reshape(n_exp, 1, 2 * D_FF)
    wd_all = w_down.reshape(n_exp, D_FF, d)
    bd_all = b_down.reshape(n_exp, 1, d)

    cs, ns, ms = [], [], []
    k_new = v_new = kv16 = None
    for l in range(DEPTH):
        rwh, rwl = _hi_lo(_pad_cols(router_w[l], LANES))
        rb = _pad_cols(router_b[l][None, :], LANES)
        wo = w_out[l].astype(BF16)
        if l < N_A:
            w = w_in_a[l]
            wg = jnp.transpose(w[:, :4 * MAIN_WIDTH].reshape(d, 4 * A_HEADS, hd), (1, 0, 2)).astype(BF16)
            wgh, wgl = _hi_lo(_pad_cols(w[:, 4 * MAIN_WIDTH:4 * MAIN_WIDTH + 2 * A_HEADS], LANES))
            wqm = w[:, 4 * MAIN_WIDTH + 2 * A_HEADS:].astype(BF16)
            p, qm, gates = proj_a(x, wg, wqm, wgh, wgl, tm=tm)
            gates = gates[:, :2 * A_HEADS] + b_gate_a[l][None, :]
            g_p = jnp.transpose(gates[:n_p].reshape(bp, seq, 2 * A_HEADS), (0, 2, 1))
            g_s = jnp.transpose(gates[n_p:].reshape(bs, dseq, 2 * A_HEADS), (0, 2, 1))
            zc = jnp.zeros((bp, A_HEADS, hd, hd), F32)
            znm = jnp.zeros((bp, A_HEADS, 8, 2 * LANES), F32)
            h_p, c_p, nm_p = mlstm(p, g_p, zc, znm, batch=bp, seq=seq, row_base=0,
                                   rows_per_step=4 * MLSTM_CHUNK, chunk=MLSTM_CHUNK)
            h_s, c_s, nm_s = mlstm(p, g_s, state_mlstm_C[l], pack_nm(state_mlstm_n[l], state_mlstm_m[l]),
                                   batch=bs, seq=dseq, row_base=n_p, rows_per_step=dseq, chunk=dseq)
            cs.append((c_p, c_s))
            ns.append((nm_p[:, :, 0, :hd], nm_s[:, :, 0, :hd]))
            ms.append((nm_p[:, :, 1, 0], nm_s[:, :, 1, 0]))
            hm_p = mem_attend(qm, 0, p_mem_k[l], p_mem_v[l], batch=bp, seq=seq, row_base=0, tq=512)
            hm_s = mem_attend(qm, 0, s_mem_k[l], s_mem_v[l], batch=bs, seq=dseq, row_base=n_p, tq=dseq)
            wo_main = wo[:MAIN_WIDTH].reshape(A_HEADS, hd, d)
            head_major = True
        else:
            if l == N_A:
                kv32, kv16 = matmul(x, w_kv_b.astype(BF16), tm=tm, tn=512, out_dtypes=(F32, BF16))
                k_new = kv32[:, :MAIN_WIDTH]
                v_new = kv32[:, MAIN_WIDTH:]
            (pq,) = matmul(x, w_in_b[l - N_A].astype(BF16), tm=tm, tn=512)
            h_p = sb_prompt(pq, kv16, batch=bp, seq=seq, tq=256, heads=4)
            q_s = pq[n_p:, :MAIN_WIDTH].reshape(bs, dseq, B_HEADS, B_HEAD_DIM) * (B_HEAD_DIM ** -0.5)
            qbd = jnp.einsum("bthd,hg->bhdgt", q_s, jnp.eye(B_HEADS, dtype=F32))
            qbd = qbd.reshape(bs, MAIN_WIDTH, B_HEADS * dseq).astype(BF16)
            h_s = sb_sample(qbd, k_new[n_p:].reshape(bs, dseq, MAIN_WIDTH),
                            v_new[n_p:].reshape(bs, dseq, MAIN_WIDTH), k_cache, v_cache,
                            tblk=1024, tsub=256).reshape(n_s, MAIN_WIDTH)
            hm_p = mem_attend(pq, MAIN_WIDTH // MEM_WIDTH, p_mem_k[l], p_mem_v[l],
                              batch=bp, seq=seq, row_base=0, tq=512)
            hm_s = mem_attend(pq, MAIN_WIDTH // MEM_WIDTH, s_mem_k[l], s_mem_v[l],
                              batch=bs, seq=dseq, row_base=n_p, tq=dseq)
            wo_main = wo[:MAIN_WIDTH]
            head_major = False
        x1, x1b, meta, metat, stats = mixer_tail(
            x, h_p, h_s, hm_p, hm_s, wo_main, wo[MAIN_WIDTH:], ln_g[l, 0][None, :], ln_b[l, 0][None, :],
            rwh, rwl, rb, head_major=head_major)
        x = moe_layer(x1, x1b, meta, metat, stats, ln_g[l, 1][None, :], ln_b[l, 1][None, :],
                      wgu_all, bgu_all, wd_all, bd_all, l)

    y_prompt = x[:n_p].reshape(bp, seq, d)
    y_sample = x[n_p:].reshape(bs, dseq, d)
    p_c = jnp.stack([c[0] for c in cs])
    s_c = jnp.stack([c[1] for c in cs])
    p_n = jnp.stack([n[0] for n in ns])
    s_n = jnp.stack([n[1] for n in ns])
    p_m = jnp.stack([m[0] for m in ms])
    s_m = jnp.stack([m[1] for m in ms])
    p_sb_k = k_new[:n_p].reshape(bp, seq, B_HEADS, B_HEAD_DIM)
    p_sb_v = v_new[:n_p].reshape(bp, seq, B_HEADS, B_HEAD_DIM)
    s_sb_k = k_new[n_p:].reshape(bs, dseq, B_HEADS, B_HEAD_DIM)
    s_sb_v = v_new[n_p:].reshape(bs, dseq, B_HEADS, B_HEAD_DIM)
    pmk = p_mem_k.reshape(DEPTH, bp, n_mem, MEM_HEADS, MEM_HEAD_DIM)
    pmv = p_mem_v.reshape(DEPTH, bp, n_mem, MEM_HEADS, MEM_HEAD_DIM)
    return (y_prompt, y_sample, p_c, p_n, p_m, p_sb_k, p_sb_v, pmk, pmv, s_c, s_n, s_m, s_sb_k, s_sb_v)
```

```python
import functools

import jax
import jax.numpy as jnp
from jax import lax
from jax.experimental import pallas as pl
from jax.experimental.pallas import tpu as pltpu

F32 = jnp.float32
BF16 = jnp.bfloat16
I32 = jnp.int32

D_MODEL = 1024
DEPTH = 4
N_A = DEPTH // 2
MEM_WIDTH = D_MODEL // 4
MAIN_WIDTH = D_MODEL - MEM_WIDTH
MEM_HEADS = 4
MEM_HEAD_DIM = MEM_WIDTH // MEM_HEADS
A_HEADS = 4
A_HEAD_DIM = MAIN_WIDTH // A_HEADS
B_HEADS = 12
B_HEAD_DIM = MAIN_WIDTH // B_HEADS
N_EXPERTS = 32
TOP_K = 4
D_FF = D_MODEL
SWIGLU_LIMIT = 7.0
SWIGLU_ALPHA = 1.702
DN_ALPHA = (2.0 * DEPTH) ** 0.25
LN_EPS = 1e-5
LOG2E = 1.4426950408889634

LANES = 128
PACK_ROWS = 16
TOK_TILE = 512
EXP_TILE = 512
MOE_CHUNK = 512
MLSTM_CHUNK = 128
VMEM_LIMIT = 56 * 1024 * 1024


def _cparams(*sem):
    return pltpu.CompilerParams(dimension_semantics=sem, vmem_limit_bytes=VMEM_LIMIT)


def _dot(a, b):
    return jnp.dot(a, b, preferred_element_type=F32)


def _dot_nt(a, b):
    return lax.dot_general(a, b, (((1,), (1,)), ((), ())), preferred_element_type=F32)


def _dot_tn(a, b):
    return lax.dot_general(a, b, (((0,), (0,)), ((), ())), preferred_element_type=F32)


def _split_bf16(x):
    hi = x.astype(BF16)
    lo = (x - hi.astype(F32)).astype(BF16)
    return hi, lo


def _softplus(x):
    return jnp.maximum(x, 0.0) + jnp.log(1.0 + jnp.exp(-jnp.abs(x)))


def _neg_abs(x):
    bits = lax.bitcast_convert_type(x, jnp.uint32) | jnp.uint32(0x80000000)
    return lax.bitcast_convert_type(bits, F32)


def _mm_kernel(x_ref, w_ref, *o_refs):
    y = _dot(x_ref[...].astype(BF16), w_ref[...])
    for o_ref in o_refs:
        o_ref[...] = y.astype(o_ref.dtype)


def matmul(x, w, *, tm, tn, out_dtypes=(F32,)):
    m, k = x.shape
    n = w.shape[1]
    outs = pl.pallas_call(
        _mm_kernel,
        grid=(m // tm, n // tn),
        in_specs=[pl.BlockSpec((tm, k), lambda i, j: (i, 0)),
                  pl.BlockSpec((k, tn), lambda i, j: (0, j))],
        out_specs=[pl.BlockSpec((tm, tn), lambda i, j: (i, j)) for _ in out_dtypes],
        out_shape=[jax.ShapeDtypeStruct((m, n), dt) for dt in out_dtypes],
        compiler_params=_cparams("parallel", "arbitrary"),
        name="matmul",
    )(x, w)
    return outs


def _proj_a_kernel(x_ref, wg_ref, wqm_ref, wgh_ref, wgl_ref, p_ref, qm_ref, gate_ref):
    x = x_ref[...]
    xh, xl = _split_bf16(x)
    for g in range(wg_ref.shape[0]):
        p_ref[g] = _dot(xh, wg_ref[g])

    @pl.when(pl.program_id(1) == 0)
    def _():
        qm_ref[...] = _dot(xh, wqm_ref[...])
        gate_ref[...] = _dot(xh, wgh_ref[...]) + _dot(xl, wgh_ref[...]) + _dot(xh, wgl_ref[...])


def proj_a(x, wg, wqm, wgh, wgl, *, tm):
    m, k = x.shape
    g, _, hd = wg.shape
    gs = A_HEADS
    return pl.pallas_call(
        _proj_a_kernel,
        grid=(m // tm, g // gs),
        in_specs=[pl.BlockSpec((tm, k), lambda i, j: (i, 0)),
                  pl.BlockSpec((gs, k, hd), lambda i, j: (j, 0, 0)),
                  pl.BlockSpec((k, MEM_WIDTH), lambda i, j: (0, 0)),
                  pl.BlockSpec((k, LANES), lambda i, j: (0, 0)),
                  pl.BlockSpec((k, LANES), lambda i, j: (0, 0))],
        out_specs=[pl.BlockSpec((gs, tm, hd), lambda i, j: (j, i, 0)),
                   pl.BlockSpec((tm, MEM_WIDTH), lambda i, j: (i, 0)),
                   pl.BlockSpec((tm, LANES), lambda i, j: (i, 0))],
        out_shape=[jax.ShapeDtypeStruct((g, m, hd), F32),
                   jax.ShapeDtypeStruct((m, MEM_WIDTH), F32),
                   jax.ShapeDtypeStruct((m, LANES), F32)],
        compiler_params=_cparams("parallel", "arbitrary"),
        name="proj_a",
    )(x, wg, wqm, wgh, wgl)


def _mlstm_kernel(q_ref, k_ref, v_ref, o_ref, g_ref, c0_ref, nm0_ref,
                  h_ref, cn_ref, nmn_ref, c_s, nm_s, *, chunk, n_chunks, heads, hd):
    step = pl.program_id(1)
    L = chunk

    @pl.when(step == 0)
    def _():
        c_s[...] = c0_ref[0]
        nm_s[...] = nm0_ref[0]

    row = lax.broadcasted_iota(I32, (L, L), 0)
    col = lax.broadcasted_iota(I32, (L, L), 1)
    eye = row == col
    tril = col <= row
    scale = hd ** -0.5

    for ch in range(n_chunks):
        rows = slice(ch * L, (ch + 1) * L)
        for h in range(heads):
            q = q_ref[h, rows, :]
            k = k_ref[h, rows, :] * scale
            v = v_ref[h, rows, :]
            li = g_ref[0, h:h + 1, rows]
            lf = -_softplus(-g_ref[0, heads + h:heads + h + 1, rows])
            lf_b = jnp.broadcast_to(lf, (L, L))
            b_col = jnp.sum(jnp.where(tril, lf_b, 0.0), axis=1, keepdims=True)
            lf_col = jnp.sum(jnp.where(eye, lf_b, 0.0), axis=1, keepdims=True)
            b_row = jnp.sum(jnp.where(row <= col, jnp.broadcast_to(lf_col, (L, L)), 0.0),
                            axis=0, keepdims=True)
            m_prev = nm_s[h, 1:2, 0:1]
            n_row = nm_s[h, 0:1, 0:hd]
            c_old = c_s[h]
            d = jnp.where(tril, b_col - b_row + li, -jnp.inf)
            inter = b_col + m_prev
            m_t = jnp.maximum(inter, jnp.max(d, axis=1, keepdims=True))
            g = jnp.exp(inter - m_t)
            qb = q.astype(BF16)
            kb = k.astype(BF16)
            vb = v.astype(BF16)
            s = _dot_nt(qb, kb) * jnp.exp(d - m_t)
            num = g * _dot(qb, c_old.astype(BF16)) + _dot(s.astype(BF16), vb)
            den = g * jnp.sum(q * n_row, axis=1, keepdims=True) + jnp.sum(s, axis=1, keepdims=True)
            hh = num / jnp.maximum(jnp.abs(den), jnp.exp(-m_t))
            h_ref[h, rows, :] = hh * jax.nn.sigmoid(o_ref[h, rows, :])

            b_last = b_col[L - 1:L, :]
            dec = b_last - b_row + li
            m_new = jnp.maximum(b_last + m_prev, jnp.max(dec, axis=1, keepdims=True))
            g_c = jnp.exp(b_last + m_prev - m_new)
            w_row = jnp.exp(dec - m_new)
            w_col = jnp.sum(jnp.where(eye, jnp.broadcast_to(w_row, (L, L)), 0.0), axis=1, keepdims=True)
            kw = k * w_col
            c_s[h] = g_c * c_old + _dot_tn(kw.astype(BF16), vb)
            nm_s[h, 0:1, 0:hd] = g_c * n_row + jnp.sum(kw, axis=0, keepdims=True)
            nm_s[h, 1:2, :] = jnp.broadcast_to(m_new, (1, nm_s.shape[2]))

    @pl.when(step == pl.num_programs(1) - 1)
    def _():
        cn_ref[0] = c_s[...]
        nmn_ref[0] = nm_s[...]


def mlstm(p, gates, c0, nm0, *, batch, seq, row_base, rows_per_step, chunk):
    hd = p.shape[-1]
    heads = A_HEADS
    steps = seq // rows_per_step
    base = row_base // rows_per_step
    nmw = nm0.shape[-1]

    def tok_map(grp):
        return lambda b, c: (grp, base + b * steps + c, 0)

    kern = functools.partial(_mlstm_kernel, chunk=chunk, n_chunks=rows_per_step // chunk,
                             heads=heads, hd=hd)
    return pl.pallas_call(
        kern,
        grid=(batch, steps),
        in_specs=[pl.BlockSpec((heads, rows_per_step, hd), tok_map(0)),
                  pl.BlockSpec((heads, rows_per_step, hd), tok_map(1)),
                  pl.BlockSpec((heads, rows_per_step, hd), tok_map(2)),
                  pl.BlockSpec((heads, rows_per_step, hd), tok_map(3)),
                  pl.BlockSpec((1, 2 * heads, rows_per_step), lambda b, c: (b, 0, c)),
                  pl.BlockSpec((1, heads, hd, hd), lambda b, c: (b, 0, 0, 0)),
                  pl.BlockSpec((1, heads, 8, nmw), lambda b, c: (b, 0, 0, 0))],
        out_specs=[pl.BlockSpec((heads, rows_per_step, hd), lambda b, c: (0, b * steps + c, 0)),
                   pl.BlockSpec((1, heads, hd, hd), lambda b, c: (b, 0, 0, 0)),
                   pl.BlockSpec((1, heads, 8, nmw), lambda b, c: (b, 0, 0, 0))],
        out_shape=[jax.ShapeDtypeStruct((heads, batch * seq, hd), F32),
                   jax.ShapeDtypeStruct((batch, heads, hd, hd), F32),
                   jax.ShapeDtypeStruct((batch, heads, 8, nmw), F32)],
        scratch_shapes=[pltpu.VMEM((heads, hd, hd), F32), pltpu.VMEM((heads, 8, nmw), F32)],
        compiler_params=_cparams("parallel", "arbitrary"),
        name="mlstm",
    )(p, p, p, p, gates, c0, nm0)


def _mem_kernel(q_ref, mk_ref, mv_ref, o_ref):
    q = q_ref[...]
    mk = mk_ref[0].astype(BF16)
    mv = mv_ref[0].astype(BF16)
    lane = lax.broadcasted_iota(I32, q.shape, 1)
    out = jnp.zeros(q.shape, F32)
    for h in range(MEM_HEADS):
        sel = (lane >= h * MEM_HEAD_DIM) & (lane < (h + 1) * MEM_HEAD_DIM)
        qh = jnp.where(sel, q, 0.0).astype(BF16)
        s = _dot_nt(qh, mk) * (MEM_HEAD_DIM ** -0.5)
        s = s - jnp.max(s, axis=1, keepdims=True)
        e = jnp.exp(s)
        p = e / jnp.sum(e, axis=1, keepdims=True)
        out = jnp.where(sel, _dot(p.astype(BF16), mv), out)
    o_ref[...] = out


def mem_attend(qsrc, col_block, mk, mv, *, batch, seq, row_base, tq):
    steps = seq // tq
    base = row_base // tq
    return pl.pallas_call(
        _mem_kernel,
        grid=(batch, steps),
        in_specs=[pl.BlockSpec((tq, MEM_WIDTH), lambda b, i: (base + b * steps + i, col_block)),
                  pl.BlockSpec((1,) + mk.shape[1:], lambda b, i: (b, 0, 0)),
                  pl.BlockSpec((1,) + mv.shape[1:], lambda b, i: (b, 0, 0))],
        out_specs=pl.BlockSpec((tq, MEM_WIDTH), lambda b, i: (b * steps + i, 0)),
        out_shape=jax.ShapeDtypeStruct((batch * seq, MEM_WIDTH), F32),
        compiler_params=_cparams("parallel", "arbitrary"),
        name="mem_attend",
    )(qsrc, mk, mv)


def _sbp_kernel(q_ref, k_ref, v_ref, o_ref, w_s, tot_s, car_s, acc_s, *, tq, heads):
    qi = pl.program_id(2)
    qn = q_ref[...] * (-(B_HEAD_DIM ** -0.5) * LOG2E)
    lane = lax.broadcasted_iota(I32, (tq, LANES), 1)
    first = lane < B_HEAD_DIM
    qs = []
    for p in range(heads // 2):
        qp = qn[:, p * LANES:(p + 1) * LANES]
        qs += [jnp.where(first, qp, 0.0).astype(BF16), jnp.where(first, 0.0, qp).astype(BF16)]
    r = lax.broadcasted_iota(I32, (tq, tq), 0)
    c = lax.broadcasted_iota(I32, (tq, tq), 1)
    tri = jnp.where(r >= c, 1.0, 0.0).astype(BF16)
    causal = c < r

    def cols(h):
        return slice((h // 2) * LANES, (h // 2 + 1) * LANES)

    def stage_x(j, masked):
        start = pl.multiple_of(j * tq, tq)
        for h in range(heads):
            zn = _dot_nt(qs[h], k_ref[pl.ds(start, tq), cols(h)])
            l1m = jnp.minimum(zn, 0.0) - jnp.log2(1.0 + jnp.exp2(_neg_abs(zn)))
            if masked:
                l1m = jnp.where(causal, l1m, 0.0)
            cin = _dot(l1m.astype(BF16), tri)
            w = cin - zn
            if masked:
                w = jnp.where(causal, w, -jnp.inf)
            w_s[h] = w
            tot_s[h] = cin[:, 0:1]

    def stage_y(j):
        start = pl.multiple_of(j * tq, tq)
        for h in range(heads):
            car = car_s[h]
            a = jnp.exp2(w_s[h] + car)
            acc_s[h] += _dot(a.astype(BF16), v_ref[pl.ds(start, tq), cols(h)])
            car_s[h] = car + tot_s[h]

    acc_s[...] = jnp.zeros(acc_s.shape, F32)
    car_s[...] = jnp.zeros(car_s.shape, F32)
    stage_x(qi, True)

    def body(it, _):
        j = qi - 1 - it
        stage_y(j + 1)
        stage_x(j, False)
        return 0

    lax.fori_loop(0, qi, body, 0)
    stage_y(0)
    for p in range(heads // 2):
        o_ref[:, p * LANES:(p + 1) * LANES] = jnp.where(first, acc_s[2 * p], acc_s[2 * p + 1])


def sb_prompt(qsrc, kv, *, batch, seq, tq, heads):
    nq = seq // tq
    width = heads * B_HEAD_DIM
    groups = MAIN_WIDTH // width
    return pl.pallas_call(
        functools.partial(_sbp_kernel, tq=tq, heads=heads),
        grid=(batch, groups, nq),
        in_specs=[pl.BlockSpec((tq, width), lambda b, p, i: (b * nq + i, p)),
                  pl.BlockSpec((seq, width), lambda b, p, i: (b, p)),
                  pl.BlockSpec((seq, width), lambda b, p, i: (b, groups + p))],
        out_specs=pl.BlockSpec((tq, width), lambda b, p, i: (b * nq + i, p)),
        out_shape=jax.ShapeDtypeStruct((batch * seq, MAIN_WIDTH), F32),
        scratch_shapes=[pltpu.VMEM((heads, tq, tq), F32), pltpu.VMEM((heads, tq, 1), F32),
                        pltpu.VMEM((heads, tq, 1), F32), pltpu.VMEM((heads, tq, LANES), F32)],
        compiler_params=_cparams("parallel", "parallel", "arbitrary"),
        name="sb_prompt",
    )(qsrc, kv, kv)


def _sbs_kernel(qbd_ref, kn_ref, vn_ref, kc_ref, vc_ref, o_ref, acc_s, car_s, *, tsub, n_sub, tnew):
    j = pl.program_id(1)
    qbd = qbd_ref[0]
    width = qbd.shape[1]

    def block(k, v, tk, mask):
        r = lax.broadcasted_iota(I32, (tk, tk), 0)
        c = lax.broadcasted_iota(I32, (tk, tk), 1)
        tri = jnp.where(c >= r, 1.0, 0.0).astype(BF16)
        z = _dot(k.astype(BF16), qbd)
        l1m = -_softplus(z)
        if mask is not None:
            l1m = jnp.where(mask, l1m, 0.0)
        hi, lo = _split_bf16(l1m)
        cin = _dot(tri, hi) + _dot(tri, lo)
        a = jnp.exp(z + car_s[...] + cin)
        if mask is not None:
            a = jnp.where(mask, a, 0.0)
        car_s[...] += cin[0:1, :]
        acc_s[...] += _dot_tn(a.astype(BF16), v.astype(BF16))

    @pl.when(j == 0)
    def _():
        acc_s[...] = jnp.zeros(acc_s.shape, F32)
        car_s[...] = jnp.zeros(car_s.shape, F32)
        s_idx = lax.broadcasted_iota(I32, (tnew, width), 0)
        t_idx = lax.broadcasted_iota(I32, (tnew, width), 1) % tnew
        block(kn_ref[0], vn_ref[0], tnew, s_idx < t_idx)

    for sub in range(n_sub - 1, -1, -1):
        rows = slice(sub * tsub, (sub + 1) * tsub)
        block(kc_ref[0, rows, :], vc_ref[0, rows, :], tsub, None)

    @pl.when(j == pl.num_programs(1) - 1)
    def _():
        heads = width // tnew
        acc = acc_s[...].reshape(heads, tnew, acc_s.shape[1])
        hrow = lax.broadcasted_iota(I32, acc.shape, 0)
        hcol = lax.broadcasted_iota(I32, acc.shape, 2) // B_HEAD_DIM
        o_ref[0] = jnp.sum(jnp.where(hrow == hcol, acc, 0.0), axis=0)


def sb_sample(qbd, k_new, v_new, k_cache, v_cache, *, tblk, tsub):
    batch, past, width = k_cache.shape
    tnew = k_new.shape[1]
    nblk = past // tblk
    return pl.pallas_call(
        functools.partial(_sbs_kernel, tsub=tsub, n_sub=tblk // tsub, tnew=tnew),
        grid=(batch, nblk),
        in_specs=[pl.BlockSpec((1,) + qbd.shape[1:], lambda b, j: (b, 0, 0)),
                  pl.BlockSpec((1, tnew, width), lambda b, j: (b, 0, 0)),
                  pl.BlockSpec((1, tnew, width), lambda b, j: (b, 0, 0)),
                  pl.BlockSpec((1, tblk, width), lambda b, j: (b, nblk - 1 - j, 0)),
                  pl.BlockSpec((1, tblk, width), lambda b, j: (b, nblk - 1 - j, 0))],
        out_specs=pl.BlockSpec((1, tnew, width), lambda b, j: (b, 0, 0)),
        out_shape=jax.ShapeDtypeStruct((batch, tnew, width), F32),
        scratch_shapes=[pltpu.VMEM((qbd.shape[2], width), F32), pltpu.VMEM((1, qbd.shape[2]), F32)],
        compiler_params=_cparams("parallel", "arbitrary"),
        name="sb_sample",
    )(qbd, k_new, v_new, k_cache, v_cache)


def _layer_norm(x, g, b):
    mu = jnp.mean(x, axis=1, keepdims=True)
    xc = x - mu
    var = jnp.mean(xc * xc, axis=1, keepdims=True)
    return xc * lax.rsqrt(var + LN_EPS) * g + b


def _tail_kernel(*refs, n_main, head_major, prompt_tiles):
    x_ref = refs[0]
    mainp_refs = refs[1:1 + n_main]
    mains_refs = refs[1 + n_main:1 + 2 * n_main]
    rest = refs[1 + 2 * n_main:]
    hmp_ref, hms_ref, wo_ref, wom_ref, g_ref, b_ref, rwh_ref, rwl_ref, rb_ref = rest[:9]
    x1_ref, x1b_ref, meta_ref, metat_ref, stat_ref = rest[9:]
    t = x_ref.shape[0]
    is_sample = pl.program_id(0) >= prompt_tiles

    def pick(p_ref, s_ref):
        if head_major:
            return jnp.where(is_sample, s_ref[0], p_ref[0]).astype(BF16)
        return jnp.where(is_sample, s_ref[...], p_ref[...]).astype(BF16)

    a = _dot(jnp.where(is_sample, hms_ref[...], hmp_ref[...]).astype(BF16), wom_ref[...])
    if head_major:
        for h in range(n_main):
            a += _dot(pick(mainp_refs[h], mains_refs[h]), wo_ref[h])
    else:
        a += _dot(pick(mainp_refs[0], mains_refs[0]), wo_ref[...])
    x1 = _layer_norm(DN_ALPHA * x_ref[...] + a, g_ref[...], b_ref[...])
    x1_ref[...] = x1
    x1b_ref[...] = x1.astype(BF16)

    xh, xl = _split_bf16(x1)
    logits = _dot(xh, rwh_ref[...]) + _dot(xl, rwh_ref[...]) + _dot(xh, rwl_ref[...]) + rb_ref[...]
    lane = lax.broadcasted_iota(I32, (t, LANES), 1)
    logits = jnp.where(lane < N_EXPERTS, logits, -jnp.inf)
    lane_f = lane.astype(F32)
    vals, sels = [], []
    for _ in range(TOP_K):
        mx = jnp.max(logits, axis=1, keepdims=True)
        idx = jnp.min(jnp.where(logits == mx, lane_f, float(LANES)), axis=1, keepdims=True)
        sel = lane_f == idx
        vals.append(mx)
        sels.append(sel)
        logits = jnp.where(sel, -jnp.inf, logits)
    exps = [jnp.exp(v - vals[0]) for v in vals]
    tot = exps[0] + exps[1] + exps[2] + exps[3]
    wts = [e / tot for e in exps]

    onehot = jnp.zeros((t, LANES), F32)
    for sel in sels:
        onehot = jnp.where(sel, 1.0, onehot)
    cnt = jnp.sum(onehot, axis=0, keepdims=True)
    pc = jnp.floor((cnt + (PACK_ROWS - 1)) * (1.0 / PACK_ROWS))
    er = lax.broadcasted_iota(I32, (LANES, LANES), 0)
    ec = lax.broadcasted_iota(I32, (LANES, LANES), 1)
    before = jnp.where(er < ec, 1.0, 0.0).astype(BF16)
    off = _dot(jnp.broadcast_to(pc, (8, LANES)).astype(BF16), before)[0:1, :]
    tr = lax.broadcasted_iota(I32, (t, t), 0)
    tc = lax.broadcasted_iota(I32, (t, t), 1)
    earlier = jnp.where(tc < tr, 1.0, 0.0).astype(BF16)
    rank = _dot(earlier, onehot.astype(BF16))
    posfull = off * float(PACK_ROWS) + rank
    meta = jnp.zeros((t, LANES), F32)
    for k in range(TOP_K):
        pos_k = jnp.sum(jnp.where(sels[k], posfull, 0.0), axis=1, keepdims=True)
        meta = jnp.where(lane == k, pos_k, meta)
        meta = jnp.where(lane == TOP_K + k, wts[k], meta)
    meta_ref[...] = meta
    metat_ref[0] = meta.T[0:8, :]
    srow = lax.broadcasted_iota(I32, (8, LANES), 0)
    stat_ref[0] = jnp.where(srow == 0, pc, jnp.where(srow == 1, off, 0.0))


def mixer_tail(x, main_p, main_s, hm_p, hm_s, wo_main, wo_mem, ln_g, ln_b, rwh, rwl, rb, *, head_major):
    n_tok, d = x.shape
    t = TOK_TILE
    nt = n_tok // t
    assert hm_s.shape[0] == t and hm_p.shape[0] == (nt - 1) * t
    last_p = nt - 2
    if head_major:
        n_main, _, hd = main_p.shape
        mainp_specs = [pl.BlockSpec((1, t, hd), functools.partial(lambda h, i: (h, jnp.minimum(i, last_p), 0), h))
                       for h in range(n_main)]
        mains_specs = [pl.BlockSpec((1, t, hd), functools.partial(lambda h, i: (h, 0, 0), h))
                       for h in range(n_main)]
        wo_spec = pl.BlockSpec(wo_main.shape, lambda i: (0, 0, 0))
    else:
        n_main = 1
        mainp_specs = [pl.BlockSpec((t, MAIN_WIDTH), lambda i: (jnp.minimum(i, last_p), 0))]
        mains_specs = [pl.BlockSpec((t, MAIN_WIDTH), lambda i: (0, 0))]
        wo_spec = pl.BlockSpec(wo_main.shape, lambda i: (0, 0))
    const2 = lambda i: (0, 0)
    return pl.pallas_call(
        functools.partial(_tail_kernel, n_main=n_main, head_major=head_major, prompt_tiles=nt - 1),
        grid=(nt,),
        in_specs=[pl.BlockSpec((t, d), lambda i: (i, 0))] + mainp_specs + mains_specs + [
            pl.BlockSpec((t, MEM_WIDTH), lambda i: (jnp.minimum(i, last_p), 0)),
            pl.BlockSpec((t, MEM_WIDTH), const2),
            wo_spec,
            pl.BlockSpec(wo_mem.shape, const2),
            pl.BlockSpec((1, d), const2), pl.BlockSpec((1, d), const2),
            pl.BlockSpec((d, LANES), const2), pl.BlockSpec((d, LANES), const2),
            pl.BlockSpec((1, LANES), const2)],
        out_specs=[pl.BlockSpec((t, d), lambda i: (i, 0)),
                   pl.BlockSpec((t, d), lambda i: (i, 0)),
                   pl.BlockSpec((t, LANES), lambda i: (i, 0)),
                   pl.BlockSpec((1, 8, t), lambda i: (i, 0, 0)),
                   pl.BlockSpec((1, 8, LANES), lambda i: (i, 0, 0))],
        out_shape=[jax.ShapeDtypeStruct((n_tok, d), F32),
                   jax.ShapeDtypeStruct((n_tok, d), BF16),
                   jax.ShapeDtypeStruct((n_tok, LANES), F32),
                   jax.ShapeDtypeStruct((nt, 8, t), F32),
                   jax.ShapeDtypeStruct((nt, 8, LANES), F32)],
        compiler_params=_cparams("parallel"),
        name="mixer_tail",
    )(x, *([main_p] * n_main), *([main_s] * n_main), hm_p, hm_s, wo_main, wo_mem, ln_g, ln_b, rwh, rwl, rb)


def _tile_rows():
    need = TOK_TILE * TOP_K + N_EXPERTS * (PACK_ROWS - 1)
    return -(-need // 256) * 256


def _piece_copies(dst_ref, tile, make_copy, action, first=0, last=None):
    pieces = _tile_rows() // PACK_ROWS
    for p in range(first, pieces if last is None else last):
        g = pl.multiple_of(dst_ref[tile * pieces + p] * PACK_ROWS, PACK_ROWS)
        action(make_copy(p * PACK_ROWS, g))


def _zero_fill(ts_ref, tn_ref, nu_ref, xg_ref, zero_s, sem, action):
    def per_expert(e, _):
        def per_piece(p, _):
            g = pl.multiple_of((ts_ref[e] + p) * PACK_ROWS, PACK_ROWS)
            action(pltpu.make_async_copy(zero_s.at[pl.ds(0, PACK_ROWS)], xg_ref.at[pl.ds(g, PACK_ROWS)], sem))
            return 0

        lax.fori_loop(0, tn_ref[e], per_piece, 0)
        return 0

    lax.fori_loop(0, N_EXPERTS, per_expert, 0)

    def per_tile(i, _):
        g = pl.multiple_of(i * EXP_TILE, EXP_TILE)
        action(pltpu.make_async_copy(zero_s, xg_ref.at[pl.ds(g, EXP_TILE)], sem))
        return 0

    lax.fori_loop(nu_ref[0], xg_ref.shape[0] // EXP_TILE, per_tile, 0)


def _dispatch_kernel(dst_ref, ts_ref, tn_ref, nu_ref, metat_ref, x_ref, xg_ref,
                     xs_s, zero_s, sem, zsem):
    tile = pl.program_id(0)
    rt, t = xs_s.shape[1], x_ref.shape[0]
    buf = tile % 2

    @pl.when(tile == 0)
    def _():
        zero_s[...] = jnp.zeros(zero_s.shape, zero_s.dtype)
        _zero_fill(ts_ref, tn_ref, nu_ref, xg_ref, zero_s, zsem, lambda cp: cp.start())
        _zero_fill(ts_ref, tn_ref, nu_ref, xg_ref, zero_s, zsem, lambda cp: cp.wait())

    def copies_of(b):
        def make_copy(s, g):
            return pltpu.make_async_copy(xs_s.at[b, pl.ds(s, PACK_ROWS)], xg_ref.at[pl.ds(g, PACK_ROWS)],
                                         sem.at[b])
        return make_copy

    posr = metat_ref[0]
    x = x_ref[...]
    per_chunk = MOE_CHUNK // PACK_ROWS
    for c in range(rt // MOE_CHUNK):
        slot = (lax.broadcasted_iota(I32, (MOE_CHUNK, t), 0) + c * MOE_CHUNK).astype(F32)
        hit = slot == posr[0:1, :]
        for k in range(1, TOP_K):
            hit = hit | (slot == posr[k:k + 1, :])
        rows = slice(c * MOE_CHUNK, (c + 1) * MOE_CHUNK)
        xs_s[buf, rows, :] = _dot(jnp.where(hit, 1.0, 0.0).astype(BF16), x).astype(BF16)
        _piece_copies(dst_ref, tile, copies_of(buf), lambda cp: cp.start(),
                      c * per_chunk, (c + 1) * per_chunk)

    @pl.when(tile > 0)
    def _():
        _piece_copies(dst_ref, tile - 1, copies_of(1 - buf), lambda cp: cp.wait())

    @pl.when(tile == pl.num_programs(0) - 1)
    def _():
        _piece_copies(dst_ref, tile, copies_of(buf), lambda cp: cp.wait())


def moe_dispatch(dstp, tail_start, tail_n, nused, metat, x1b, n_rows):
    n_tok, d = x1b.shape
    t = TOK_TILE
    rt = _tile_rows()
    return pl.pallas_call(
        _dispatch_kernel,
        grid_spec=pltpu.PrefetchScalarGridSpec(
            num_scalar_prefetch=4,
            grid=(n_tok // t,),
            in_specs=[pl.BlockSpec((1, 8, t), lambda i, *_: (i, 0, 0)),
                      pl.BlockSpec((t, d), lambda i, *_: (i, 0))],
            out_specs=pl.BlockSpec(memory_space=pl.ANY),
            scratch_shapes=[pltpu.VMEM((2, rt, d), BF16), pltpu.VMEM((EXP_TILE, d), BF16),
                            pltpu.SemaphoreType.DMA((2,)), pltpu.SemaphoreType.DMA(())]),
        out_shape=jax.ShapeDtypeStruct((n_rows + 2 * rt, d), BF16),
        compiler_params=_cparams("arbitrary"),
        name="moe_dispatch",
    )(dstp, tail_start, tail_n, nused, metat, x1b)


def _expert_kernel(te_ref, nu_ref, x_ref, wgu_ref, bgu_ref, wd_ref, bd_ref, y_ref, wgu_s, wd_s):
    i = pl.program_id(0)
    prev = te_ref[jnp.maximum(i - 1, 0)]

    @pl.when((i == 0) | (te_ref[i] != prev))
    def _():
        wgu_s[...] = wgu_ref[0].astype(BF16)
        wd_s[...] = wd_ref[0].astype(BF16)

    @pl.when(i < nu_ref[0])
    def _():
        x = x_ref[...]
        f = wd_s.shape[0]
        half = f // 2
        y = jnp.broadcast_to(bd_ref[0], y_ref.shape)
        for c in range(2):
            cols = slice(c * half, (c + 1) * half)
            ucols = slice(f + c * half, f + (c + 1) * half)
            gate = jnp.minimum(_dot(x, wgu_s[:, cols]) + bgu_ref[0, :, cols], SWIGLU_LIMIT)
            up = jnp.clip(_dot(x, wgu_s[:, ucols]) + bgu_ref[0, :, ucols], -SWIGLU_LIMIT, SWIGLU_LIMIT)
            act = (up + 1.0) * gate * jax.nn.sigmoid(SWIGLU_ALPHA * gate)
            y = y + _dot(act.astype(BF16), wd_s[cols, :])
        y_ref[...] = y.astype(y_ref.dtype)

    @pl.when(i >= nu_ref[0])
    def _():
        y_ref[...] = jnp.zeros(y_ref.shape, y_ref.dtype)


def moe_experts(te, nused, xg, wgu, bgu, wd, bd, n_rows):
    d = xg.shape[1]
    f2 = wgu.shape[-1]
    tm = EXP_TILE

    def row_map(i, te_ref, nu_ref):
        return (jnp.minimum(i, nu_ref[0] - 1), 0)

    def w_map(i, te_ref, nu_ref):
        return (te_ref[i], 0, 0)

    return pl.pallas_call(
        _expert_kernel,
        grid_spec=pltpu.PrefetchScalarGridSpec(
            num_scalar_prefetch=2,
            grid=(n_rows // tm,),
            in_specs=[pl.BlockSpec((tm, d), row_map),
                      pl.BlockSpec((1, d, f2), w_map),
                      pl.BlockSpec((1, 1, f2), w_map),
                      pl.BlockSpec((1, f2 // 2, d), w_map),
                      pl.BlockSpec((1, 1, d), w_map)],
            out_specs=pl.BlockSpec((tm, d), lambda i, *_: (i, 0)),
            scratch_shapes=[pltpu.VMEM((d, f2), BF16), pltpu.VMEM((f2 // 2, d), BF16)]),
        out_shape=jax.ShapeDtypeStruct((n_rows, d), BF16),
        compiler_params=_cparams("arbitrary"),
        name="moe_experts",
    )(te, nused, xg, wgu, bgu, wd, bd)


def _combine_kernel(dst_ref, meta_ref, x1_ref, g_ref, b_ref, yg_ref, x2_ref, ys_s, sem):
    tile = pl.program_id(0)
    last = pl.num_programs(0) - 1
    rt, t = ys_s.shape[1], x1_ref.shape[0]
    buf = tile % 2

    def copies_of(b):
        def make_copy(s, g):
            return pltpu.make_async_copy(yg_ref.at[pl.ds(g, PACK_ROWS)], ys_s.at[b, pl.ds(s, PACK_ROWS)],
                                         sem.at[b])
        return make_copy

    @pl.when(tile == 0)
    def _():
        _piece_copies(dst_ref, tile, copies_of(buf), lambda cp: cp.start())

    _piece_copies(dst_ref, tile, copies_of(buf), lambda cp: cp.wait())
    nxt = jnp.minimum(tile + 1, last)
    meta = meta_ref[...]
    f = jnp.zeros(x1_ref.shape, F32)
    per_chunk = MOE_CHUNK // PACK_ROWS
    for c in range(rt // MOE_CHUNK):
        _piece_copies(dst_ref, nxt, copies_of(1 - buf), lambda cp: cp.start(),
                      c * per_chunk, (c + 1) * per_chunk)
        slot = (lax.broadcasted_iota(I32, (t, MOE_CHUNK), 1) + c * MOE_CHUNK).astype(F32)
        wmat = jnp.zeros((t, MOE_CHUNK), F32)
        for k in range(TOP_K):
            wmat = jnp.where(slot == meta[:, k:k + 1], meta[:, TOP_K + k:TOP_K + k + 1], wmat)
        f += _dot(wmat.astype(BF16), ys_s[buf, c * MOE_CHUNK:(c + 1) * MOE_CHUNK, :])
    x2_ref[...] = _layer_norm(DN_ALPHA * x1_ref[...] + f, g_ref[...], b_ref[...])

    @pl.when(tile == last)
    def _():
        _piece_copies(dst_ref, tile, copies_of(1 - buf), lambda cp: cp.wait())


def moe_combine(dstp, meta, x1, ln_g, ln_b, yg):
    n_tok, d = x1.shape
    t = TOK_TILE
    rt = _tile_rows()
    return pl.pallas_call(
        _combine_kernel,
        grid_spec=pltpu.PrefetchScalarGridSpec(
            num_scalar_prefetch=1,
            grid=(n_tok // t,),
            in_specs=[pl.BlockSpec((t, LANES), lambda i, *_: (i, 0)),
                      pl.BlockSpec((t, d), lambda i, *_: (i, 0)),
                      pl.BlockSpec((1, d), lambda i, *_: (0, 0)),
                      pl.BlockSpec((1, d), lambda i, *_: (0, 0)),
                      pl.BlockSpec(memory_space=pl.ANY)],
            out_specs=pl.BlockSpec((t, d), lambda i, *_: (i, 0)),
            scratch_shapes=[pltpu.VMEM((2, rt, d), BF16), pltpu.SemaphoreType.DMA((2,))]),
        out_shape=jax.ShapeDtypeStruct((n_tok, d), F32),
        compiler_params=_cparams("arbitrary"),
        name="moe_combine",
    )(dstp, meta, x1, ln_g, ln_b, yg)


def _route_tables(stats, n_rows):
    pc = stats[:, 0, :N_EXPERTS].astype(I32)
    off = stats[:, 1, :N_EXPERTS].astype(I32)
    pieces_e = jnp.sum(pc, axis=0)
    per_tile = EXP_TILE // PACK_ROWS
    tiles_e = (pieces_e + per_tile - 1) // per_tile
    ends = jnp.cumsum(tiles_e)
    base_e = (ends - tiles_e) * per_tile
    gd = base_e[None, :] + jnp.cumsum(pc, axis=0) - pc
    tail_start = base_e + pieces_e
    tail_n = tiles_e * per_tile - pieces_e
    nused = ends[-1]
    tile_ids = jnp.minimum(jnp.arange(n_rows // EXP_TILE, dtype=I32), nused - 1)
    te = jnp.sum((ends[None, :] <= tile_ids[:, None]).astype(I32), axis=1)
    seg_end = off + pc
    pidx = jnp.arange(_tile_rows() // PACK_ROWS, dtype=I32)
    e_of_p = jnp.sum((seg_end[:, None, :] <= pidx[None, :, None]).astype(I32), axis=2)
    sel = e_of_p[:, :, None] == jnp.arange(N_EXPERTS, dtype=I32)[None, None, :]
    dstp = jnp.sum(jnp.where(sel, (gd - off)[:, None, :], 0), axis=2) + pidx[None, :]
    used = pidx[None, :] < seg_end[:, -1:]
    spare = (n_rows // PACK_ROWS + (jnp.arange(pc.shape[0], dtype=I32) % 2)[:, None] * pidx.shape[0]
             + pidx[None, :])
    dst_disp = jnp.where(used, dstp, spare)
    dst_comb = jnp.where(used, dstp, 0)
    return (dst_disp.reshape(-1).astype(I32), dst_comb.reshape(-1).astype(I32), tail_start.astype(I32),
            tail_n.astype(I32), te, nused.reshape(1).astype(I32))


def moe_layer(x1, x1b, meta, metat, stats, ln_g, ln_b, wgu, bgu, wd, bd, layer):
    n_tok = x1.shape[0]
    nt = n_tok // TOK_TILE
    worst = n_tok * TOP_K + nt * N_EXPERTS * (PACK_ROWS - 1) + N_EXPERTS * (EXP_TILE - 1)
    n_rows = -(-worst // EXP_TILE) * EXP_TILE
    dst_disp, dst_comb, tail_start, tail_n, te, nused = _route_tables(stats, n_rows)
    xg = moe_dispatch(dst_disp, tail_start, tail_n, nused, metat, x1b, n_rows)
    yg = moe_experts(te + layer * N_EXPERTS, nused, xg, wgu, bgu, wd, bd, n_rows)
    return moe_combine(dst_comb, meta, x1, ln_g, ln_b, yg)


def _pad_cols(w, width):
    return jnp.pad(w, ((0, 0), (0, width - w.shape[1])))


def _hi_lo(w):
    hi = w.astype(BF16)
    return hi, (w - hi.astype(F32)).astype(BF16)


def kernel(x_prompt, x_sample, state_mlstm_C, state_mlstm_n, state_mlstm_m, cache_sb_k, cache_sb_v,
           cache_mem_k, cache_mem_v, mem_prompt, w_in_a, b_gate_a, w_in_b, w_kv_b, w_mem_kv, w_out,
           ln_g, ln_b, router_w, router_b, w_gate_up, b_gate_up, w_down, b_down):
    bp, seq, d = x_prompt.shape
    bs, dseq, _ = x_sample.shape
    n_mem = mem_prompt.shape[1]
    past = cache_sb_k.shape[1]
    n_p = bp * seq
    n_s = bs * dseq
    hd = A_HEAD_DIM
    x = jnp.concatenate([x_prompt.reshape(n_p, d), x_sample.reshape(n_s, d)], axis=0)
    n_tok = n_p + n_s
    tm = n_tok // 26 if n_tok % 26 == 0 and (n_tok // 26) % 8 == 0 else TOK_TILE

    wm = jnp.transpose(w_mem_kv, (1, 0, 2)).reshape(d, DEPTH * 2 * MEM_WIDTH).astype(BF16)
    (mkv,) = matmul(mem_prompt.reshape(bp * n_mem, d), wm, tm=512, tn=512)
    mkv = mkv.reshape(bp, n_mem, DEPTH, 2, MEM_WIDTH)
    p_mem_k = jnp.transpose(mkv[:, :, :, 0], (2, 0, 1, 3))
    p_mem_v = jnp.transpose(mkv[:, :, :, 1], (2, 0, 1, 3))
    s_mem_k = cache_mem_k.reshape(DEPTH, bs, n_mem, MEM_WIDTH)
    s_mem_v = cache_mem_v.reshape(DEPTH, bs, n_mem, MEM_WIDTH)

    def pack_nm(n, m):
        b = n.shape[0]
        out = jnp.zeros((b, A_HEADS, 8, 2 * LANES), F32)
        out = out.at[:, :, 0, :hd].set(n)
        return out.at[:, :, 1, :].set(jnp.broadcast_to(m[:, :, None], (b, A_HEADS, 2 * LANES)))

    k_cache = cache_sb_k.reshape(bs, past, MAIN_WIDTH)
    v_cache = cache_sb_v.reshape(bs, past, MAIN_WIDTH)
    n_exp = DEPTH * N_EXPERTS
    wgu_all = w_gate_up.reshape(n_exp, d, 2 * D_FF)
    bgu_all = b_gate_up.reshape(n_exp, 1, 2 * D_FF)
    wd_all = w_down.reshape(n_exp, D_FF, d)
    bd_all = b_down.reshape(n_exp, 1, d)

    cs, ns, ms = [], [], []
    k_new = v_new = kv16 = None
    for l in range(DEPTH):
        rwh, rwl = _hi_lo(_pad_cols(router_w[l], LANES))
        rb = _pad_cols(router_b[l][None, :], LANES)
        wo = w_out[l].astype(BF16)
        if l < N_A:
            w = w_in_a[l]
            wg = jnp.transpose(w[:, :4 * MAIN_WIDTH].reshape(d, 4 * A_HEADS, hd), (1, 0, 2)).astype(BF16)
            wgh, wgl = _hi_lo(_pad_cols(w[:, 4 * MAIN_WIDTH:4 * MAIN_WIDTH + 2 * A_HEADS], LANES))
            wqm = w[:, 4 * MAIN_WIDTH + 2 * A_HEADS:].astype(BF16)
            p, qm, gates = proj_a(x, wg, wqm, wgh, wgl, tm=tm)
            gates = gates[:, :2 * A_HEADS] + b_gate_a[l][None, :]
            g_p = jnp.transpose(gates[:n_p].reshape(bp, seq, 2 * A_HEADS), (0, 2, 1))
            g_s = jnp.transpose(gates[n_p:].reshape(bs, dseq, 2 * A_HEADS), (0, 2, 1))
            zc = jnp.zeros((bp, A_HEADS, hd, hd), F32)
            znm = jnp.zeros((bp, A_HEADS, 8, 2 * LANES), F32)
            h_p, c_p, nm_p = mlstm(p, g_p, zc, znm, batch=bp, seq=seq, row_base=0,
                                   rows_per_step=4 * MLSTM_CHUNK, chunk=MLSTM_CHUNK)
            h_s, c_s, nm_s = mlstm(p, g_s, state_mlstm_C[l], pack_nm(state_mlstm_n[l], state_mlstm_m[l]),
                                   batch=bs, seq=dseq, row_base=n_p, rows_per_step=dseq, chunk=dseq)
            cs.append((c_p, c_s))
            ns.append((nm_p[:, :, 0, :hd], nm_s[:, :, 0, :hd]))
            ms.append((nm_p[:, :, 1, 0], nm_s[:, :, 1, 0]))
            hm_p = mem_attend(qm, 0, p_mem_k[l], p_mem_v[l], batch=bp, seq=seq, row_base=0, tq=512)
            hm_s = mem_attend(qm, 0, s_mem_k[l], s_mem_v[l], batch=bs, seq=dseq, row_base=n_p, tq=dseq)
            wo_main = wo[:MAIN_WIDTH].reshape(A_HEADS, hd, d)
            head_major = True
        else:
            if l == N_A:
                kv32, kv16 = matmul(x, w_kv_b.astype(BF16), tm=tm, tn=512, out_dtypes=(F32, BF16))
                k_new = kv32[:, :MAIN_WIDTH]
                v_new = kv32[:, MAIN_WIDTH:]
            (pq,) = matmul(x, w_in_b[l - N_A].astype(BF16), tm=tm, tn=512)
            h_p = sb_prompt(pq, kv16, batch=bp, seq=seq, tq=256, heads=4)
            q_s = pq[n_p:, :MAIN_WIDTH].reshape(bs, dseq, B_HEADS, B_HEAD_DIM) * (B_HEAD_DIM ** -0.5)
            qbd = jnp.einsum("bthd,hg->bhdgt", q_s, jnp.eye(B_HEADS, dtype=F32))
            qbd = qbd.reshape(bs, MAIN_WIDTH, B_HEADS * dseq).astype(BF16)
            h_s = sb_sample(qbd, k_new[n_p:].reshape(bs, dseq, MAIN_WIDTH),
                            v_new[n_p:].reshape(bs, dseq, MAIN_WIDTH), k_cache, v_cache,
                            tblk=1024, tsub=256).reshape(n_s, MAIN_WIDTH)
            hm_p = mem_attend(pq, MAIN_WIDTH // MEM_WIDTH, p_mem_k[l], p_mem_v[l],
                              batch=bp, seq=seq, row_base=0, tq=512)
            hm_s = mem_attend(pq, MAIN_WIDTH // MEM_WIDTH, s_mem_k[l], s_mem_v[l],
                              batch=bs, seq=dseq, row_base=n_p, tq=dseq)
            wo_main = wo[:MAIN_WIDTH]
            head_major = False
        x1, x1b, meta, metat, stats = mixer_tail(
            x, h_p, h_s, hm_p, hm_s, wo_main, wo[MAIN_WIDTH:], ln_g[l, 0][None, :], ln_b[l, 0][None, :],
            rwh, rwl, rb, head_major=head_major)
        x = moe_layer(x1, x1b, meta, metat, stats, ln_g[l, 1][None, :], ln_b[l, 1][None, :],
                      wgu_all, bgu_all, wd_all, bd_all, l)

    y_prompt = x[:n_p].reshape(bp, seq, d)
    y_sample = x[n_p:].reshape(bs, dseq, d)
    p_c = jnp.stack([c[0] for c in cs])
    s_c = jnp.stack([c[1] for c in cs])
    p_n = jnp.stack([n[0] for n in ns])
    s_n = jnp.stack([n[1] for n in ns])
    p_m = jnp.stack([m[0] for m in ms])
    s_m = jnp.stack([m[1] for m in ms])
    p_sb_k = k_new[:n_p].reshape(bp, seq, B_HEADS, B_HEAD_DIM)
    p_sb_v = v_new[:n_p].reshape(bp, seq, B_HEADS, B_HEAD_DIM)
    s_sb_k = k_new[n_p:].reshape(bs, dseq, B_HEADS, B_HEAD_DIM)
    s_sb_v = v_new[n_p:].reshape(bs, dseq, B_HEADS, B_HEAD_DIM)
    pmk = p_mem_k.reshape(DEPTH, bp, n_mem, MEM_HEADS, MEM_HEAD_DIM)
    pmv = p_mem_v.reshape(DEPTH, bp, n_mem, MEM_HEADS, MEM_HEAD_DIM)
    return (y_prompt, y_sample, p_c, p_n, p_m, p_sb_k, p_sb_v, pmk, pmv, s_c, s_n, s_m, s_sb_k, s_sb_v)
```

```python
import functools

import jax
import jax.numpy as jnp
from jax import lax
from jax.experimental import pallas as pl
from jax.experimental.pallas import tpu as pltpu

F32 = jnp.float32
BF16 = jnp.bfloat16
I32 = jnp.int32

D_MODEL = 1024
DEPTH = 4
N_A = DEPTH // 2
MEM_WIDTH = D_MODEL // 4
MAIN_WIDTH = D_MODEL - MEM_WIDTH
MEM_HEADS = 4
MEM_HEAD_DIM = MEM_WIDTH // MEM_HEADS
A_HEADS = 4
A_HEAD_DIM = MAIN_WIDTH // A_HEADS
B_HEADS = 12
B_HEAD_DIM = MAIN_WIDTH // B_HEADS
N_EXPERTS = 32
TOP_K = 4
D_FF = D_MODEL
SWIGLU_LIMIT = 7.0
SWIGLU_ALPHA = 1.702
DN_ALPHA = (2.0 * DEPTH) ** 0.25
LN_EPS = 1e-5
LOG2E = 1.4426950408889634

LANES = 128
PACK_ROWS = 16
TOK_TILE = 512
EXP_TILE = 512
MOE_CHUNK = 512
MLSTM_CHUNK = 256
VMEM_LIMIT = 56 * 1024 * 1024


def _cparams(*sem):
    return pltpu.CompilerParams(dimension_semantics=sem, vmem_limit_bytes=VMEM_LIMIT)


def _dot(a, b):
    return jnp.dot(a, b, preferred_element_type=F32)


def _dot_nt(a, b):
    return lax.dot_general(a, b, (((1,), (1,)), ((), ())), preferred_element_type=F32)


def _dot_tn(a, b):
    return lax.dot_general(a, b, (((0,), (0,)), ((), ())), preferred_element_type=F32)


def _split_bf16(x):
    hi = x.astype(BF16)
    lo = (x - hi.astype(F32)).astype(BF16)
    return hi, lo


def _softplus(x):
    return jnp.maximum(x, 0.0) + jnp.log(1.0 + jnp.exp(-jnp.abs(x)))


def _neg_abs(x):
    bits = lax.bitcast_convert_type(x, jnp.uint32) | jnp.uint32(0x80000000)
    return lax.bitcast_convert_type(bits, F32)


def _mm_kernel(x_ref, w_ref, *o_refs):
    y = _dot(x_ref[...].astype(BF16), w_ref[...])
    for o_ref in o_refs:
        o_ref[...] = y.astype(o_ref.dtype)


def matmul(x, w, *, tm, tn, out_dtypes=(F32,)):
    m, k = x.shape
    n = w.shape[1]
    outs = pl.pallas_call(
        _mm_kernel,
        grid=(m // tm, n // tn),
        in_specs=[pl.BlockSpec((tm, k), lambda i, j: (i, 0)),
                  pl.BlockSpec((k, tn), lambda i, j: (0, j))],
        out_specs=[pl.BlockSpec((tm, tn), lambda i, j: (i, j)) for _ in out_dtypes],
        out_shape=[jax.ShapeDtypeStruct((m, n), dt) for dt in out_dtypes],
        compiler_params=_cparams("parallel", "arbitrary"),
        name="matmul",
    )(x, w)
    return outs


def _proj_a_kernel(x_ref, wg_ref, wqm_ref, wgh_ref, wgl_ref, p_ref, qm_ref, gate_ref):
    x = x_ref[...]
    xh, xl = _split_bf16(x)
    for g in range(wg_ref.shape[0]):
        p_ref[g] = _dot(xh, wg_ref[g])

    @pl.when(pl.program_id(1) == 0)
    def _():
        qm_ref[...] = _dot(xh, wqm_ref[...])
        gate_ref[...] = _dot(xh, wgh_ref[...]) + _dot(xl, wgh_ref[...]) + _dot(xh, wgl_ref[...])


def proj_a(x, wg, wqm, wgh, wgl, *, tm):
    m, k = x.shape
    g, _, hd = wg.shape
    gs = A_HEADS
    return pl.pallas_call(
        _proj_a_kernel,
        grid=(m // tm, g // gs),
        in_specs=[pl.BlockSpec((tm, k), lambda i, j: (i, 0)),
                  pl.BlockSpec((gs, k, hd), lambda i, j: (j, 0, 0)),
                  pl.BlockSpec((k, MEM_WIDTH), lambda i, j: (0, 0)),
                  pl.BlockSpec((k, LANES), lambda i, j: (0, 0)),
                  pl.BlockSpec((k, LANES), lambda i, j: (0, 0))],
        out_specs=[pl.BlockSpec((gs, tm, hd), lambda i, j: (j, i, 0)),
                   pl.BlockSpec((tm, MEM_WIDTH), lambda i, j: (i, 0)),
                   pl.BlockSpec((tm, LANES), lambda i, j: (i, 0))],
        out_shape=[jax.ShapeDtypeStruct((g, m, hd), F32),
                   jax.ShapeDtypeStruct((m, MEM_WIDTH), F32),
                   jax.ShapeDtypeStruct((m, LANES), F32)],
        compiler_params=_cparams("parallel", "arbitrary"),
        name="proj_a",
    )(x, wg, wqm, wgh, wgl)


def _mlstm_kernel(q_ref, k_ref, v_ref, o_ref, g_ref, c0_ref, nm0_ref,
                  h_ref, cn_ref, nmn_ref, c_s, nm_s, *, chunk, n_chunks, heads, hd):
    step = pl.program_id(1)
    L = chunk

    @pl.when(step == 0)
    def _():
        c_s[...] = c0_ref[0]
        nm_s[...] = nm0_ref[0]

    row = lax.broadcasted_iota(I32, (L, L), 0)
    col = lax.broadcasted_iota(I32, (L, L), 1)
    eye = row == col
    tril = col <= row
    scale = hd ** -0.5

    for h in range(heads):
        local = []
        for ch in range(n_chunks):
            rows = slice(ch * L, (ch + 1) * L)
            q = q_ref[h, rows, :]
            k = k_ref[h, rows, :] * scale
            v = v_ref[h, rows, :]
            li = g_ref[0, h:h + 1, rows]
            lf = -_softplus(-g_ref[0, heads + h:heads + h + 1, rows])
            lf_b = jnp.broadcast_to(lf, (L, L))
            b_col = jnp.sum(jnp.where(tril, lf_b, 0.0), axis=1, keepdims=True)
            lf_col = jnp.sum(jnp.where(eye, lf_b, 0.0), axis=1, keepdims=True)
            b_row = jnp.sum(jnp.where(row <= col, jnp.broadcast_to(lf_col, (L, L)), 0.0),
                            axis=0, keepdims=True)
            d = jnp.where(tril, b_col - b_row + li, -jnp.inf)
            m_loc = jnp.max(d, axis=1, keepdims=True)
            qb = q.astype(BF16)
            vb = v.astype(BF16)
            s = _dot_nt(qb, k.astype(BF16)) * jnp.exp(d - m_loc)
            sv = _dot(s.astype(BF16), vb)
            rs = jnp.sum(s, axis=1, keepdims=True)
            b_last = b_col[L - 1:L, :]
            dec = b_last - b_row + li
            m_dec = jnp.max(dec, axis=1, keepdims=True)
            w_row = jnp.exp(dec - m_dec)
            w_col = jnp.sum(jnp.where(eye, jnp.broadcast_to(w_row, (L, L)), 0.0), axis=1, keepdims=True)
            kw = k * w_col
            kv = _dot_tn(kw.astype(BF16), vb)
            ks = jnp.sum(kw, axis=0, keepdims=True)
            local.append((rows, q, qb, b_col, m_loc, sv, rs, b_last, m_dec, kv, ks))

        m_prev = nm_s[h, 1:2, 0:1]
        n_row = nm_s[h, 0:1, 0:hd]
        c_old = c_s[h]
        for rows, q, qb, b_col, m_loc, sv, rs, b_last, m_dec, kv, ks in local:
            inter = b_col + m_prev
            m_t = jnp.maximum(inter, m_loc)
            g = jnp.exp(inter - m_t)
            f = jnp.exp(m_loc - m_t)
            num = g * _dot(qb, c_old.astype(BF16)) + f * sv
            den = g * jnp.sum(q * n_row, axis=1, keepdims=True) + f * rs
            hh = num / jnp.maximum(jnp.abs(den), jnp.exp(-m_t))
            h_ref[h, rows, :] = hh * jax.nn.sigmoid(o_ref[h, rows, :])
            m_new = jnp.maximum(b_last + m_prev, m_dec)
            g_c = jnp.exp(b_last + m_prev - m_new)
            f_c = jnp.exp(m_dec - m_new)
            c_old = g_c * c_old + f_c * kv
            n_row = g_c * n_row + f_c * ks
            m_prev = m_new
        c_s[h] = c_old
        nm_s[h, 0:1, 0:hd] = n_row
        nm_s[h, 1:2, :] = jnp.broadcast_to(m_prev, (1, nm_s.shape[2]))

    @pl.when(step == pl.num_programs(1) - 1)
    def _():
        cn_ref[0] = c_s[...]
        nmn_ref[0] = nm_s[...]


def mlstm(p, gates, c0, nm0, *, batch, seq, row_base, rows_per_step, chunk):
    hd = p.shape[-1]
    heads = A_HEADS
    steps = seq // rows_per_step
    base = row_base // rows_per_step
    nmw = nm0.shape[-1]

    def tok_map(grp):
        return lambda b, c: (grp, base + b * steps + c, 0)

    kern = functools.partial(_mlstm_kernel, chunk=chunk, n_chunks=rows_per_step // chunk,
                             heads=heads, hd=hd)
    return pl.pallas_call(
        kern,
        grid=(batch, steps),
        in_specs=[pl.BlockSpec((heads, rows_per_step, hd), tok_map(0)),
                  pl.BlockSpec((heads, rows_per_step, hd), tok_map(1)),
                  pl.BlockSpec((heads, rows_per_step, hd), tok_map(2)),
                  pl.BlockSpec((heads, rows_per_step, hd), tok_map(3)),
                  pl.BlockSpec((1, 2 * heads, rows_per_step), lambda b, c: (b, 0, c)),
                  pl.BlockSpec((1, heads, hd, hd), lambda b, c: (b, 0, 0, 0)),
                  pl.BlockSpec((1, heads, 8, nmw), lambda b, c: (b, 0, 0, 0))],
        out_specs=[pl.BlockSpec((heads, rows_per_step, hd), lambda b, c: (0, b * steps + c, 0)),
                   pl.BlockSpec((1, heads, hd, hd), lambda b, c: (b, 0, 0, 0)),
                   pl.BlockSpec((1, heads, 8, nmw), lambda b, c: (b, 0, 0, 0))],
        out_shape=[jax.ShapeDtypeStruct((heads, batch * seq, hd), F32),
                   jax.ShapeDtypeStruct((batch, heads, hd, hd), F32),
                   jax.ShapeDtypeStruct((batch, heads, 8, nmw), F32)],
        scratch_shapes=[pltpu.VMEM((heads, hd, hd), F32), pltpu.VMEM((heads, 8, nmw), F32)],
        compiler_params=_cparams("parallel", "arbitrary"),
        name="mlstm",
    )(p, p, p, p, gates, c0, nm0)


def _mem_kernel(q_ref, mk_ref, mv_ref, o_ref):
    q = q_ref[...]
    mk = mk_ref[0].astype(BF16)
    mv = mv_ref[0].astype(BF16)
    lane = lax.broadcasted_iota(I32, q.shape, 1)
    out = jnp.zeros(q.shape, F32)
    for h in range(MEM_HEADS):
        sel = (lane >= h * MEM_HEAD_DIM) & (lane < (h + 1) * MEM_HEAD_DIM)
        qh = jnp.where(sel, q, 0.0).astype(BF16)
        s = _dot_nt(qh, mk) * (MEM_HEAD_DIM ** -0.5)
        s = s - jnp.max(s, axis=1, keepdims=True)
        e = jnp.exp(s)
        p = e / jnp.sum(e, axis=1, keepdims=True)
        out = jnp.where(sel, _dot(p.astype(BF16), mv), out)
    o_ref[...] = out


def mem_attend(qsrc, col_block, mk, mv, *, batch, seq, row_base, tq):
    steps = seq // tq
    base = row_base // tq
    return pl.pallas_call(
        _mem_kernel,
        grid=(batch, steps),
        in_specs=[pl.BlockSpec((tq, MEM_WIDTH), lambda b, i: (base + b * steps + i, col_block)),
                  pl.BlockSpec((1,) + mk.shape[1:], lambda b, i: (b, 0, 0)),
                  pl.BlockSpec((1,) + mv.shape[1:], lambda b, i: (b, 0, 0))],
        out_specs=pl.BlockSpec((tq, MEM_WIDTH), lambda b, i: (b * steps + i, 0)),
        out_shape=jax.ShapeDtypeStruct((batch * seq, MEM_WIDTH), F32),
        compiler_params=_cparams("parallel", "arbitrary"),
        name="mem_attend",
    )(qsrc, mk, mv)


def _sbp_kernel(q_ref, k_ref, v_ref, o_ref, w_s, tot_s, car_s, acc_s, *, tq, heads):
    qi = pl.program_id(2)
    qn = q_ref[...] * (-(B_HEAD_DIM ** -0.5) * LOG2E)
    lane = lax.broadcasted_iota(I32, (tq, LANES), 1)
    first = lane < B_HEAD_DIM
    qs = []
    for p in range(heads // 2):
        qp = qn[:, p * LANES:(p + 1) * LANES]
        qs += [jnp.where(first, qp, 0.0).astype(BF16), jnp.where(first, 0.0, qp).astype(BF16)]
    r = lax.broadcasted_iota(I32, (tq, tq), 0)
    c = lax.broadcasted_iota(I32, (tq, tq), 1)
    tri = jnp.where(r >= c, 1.0, 0.0).astype(BF16)
    causal = c < r

    def cols(h):
        return slice((h // 2) * LANES, (h // 2 + 1) * LANES)

    def stage_x(j, masked):
        start = pl.multiple_of(j * tq, tq)
        for h in range(heads):
            zn = _dot_nt(qs[h], k_ref[pl.ds(start, tq), cols(h)])
            l1m = jnp.minimum(zn, 0.0) - jnp.log2(1.0 + jnp.exp2(_neg_abs(zn)))
            if masked:
                l1m = jnp.where(causal, l1m, 0.0)
            cin = _dot(l1m.astype(BF16), tri)
            w = cin - zn
            if masked:
                w = jnp.where(causal, w, -jnp.inf)
            w_s[h] = w
            tot_s[h] = cin[:, 0:1]

    def stage_y(j):
        start = pl.multiple_of(j * tq, tq)
        for h in range(heads):
            car = car_s[h]
            a = jnp.exp2(w_s[h] + car)
            acc_s[h] += _dot(a.astype(BF16), v_ref[pl.ds(start, tq), cols(h)])
            car_s[h] = car + tot_s[h]

    acc_s[...] = jnp.zeros(acc_s.shape, F32)
    car_s[...] = jnp.zeros(car_s.shape, F32)
    stage_x(qi, True)

    def body(it, _):
        j = qi - 1 - it
        stage_y(j + 1)
        stage_x(j, False)
        return 0

    lax.fori_loop(0, qi, body, 0)
    stage_y(0)
    for p in range(heads // 2):
        o_ref[:, p * LANES:(p + 1) * LANES] = jnp.where(first, acc_s[2 * p], acc_s[2 * p + 1])


def sb_prompt(qsrc, kv, *, batch, seq, tq, heads):
    nq = seq // tq
    width = heads * B_HEAD_DIM
    groups = MAIN_WIDTH // width
    return pl.pallas_call(
        functools.partial(_sbp_kernel, tq=tq, heads=heads),
        grid=(batch, groups, nq),
        in_specs=[pl.BlockSpec((tq, width), lambda b, p, i: (b * nq + i, p)),
                  pl.BlockSpec((seq, width), lambda b, p, i: (b, p)),
                  pl.BlockSpec((seq, width), lambda b, p, i: (b, groups + p))],
        out_specs=pl.BlockSpec((tq, width), lambda b, p, i: (b * nq + i, p)),
        out_shape=jax.ShapeDtypeStruct((batch * seq, MAIN_WIDTH), F32),
        scratch_shapes=[pltpu.VMEM((heads, tq, tq), F32), pltpu.VMEM((heads, tq, 1), F32),
                        pltpu.VMEM((heads, tq, 1), F32), pltpu.VMEM((heads, tq, LANES), F32)],
        compiler_params=_cparams("parallel", "parallel", "arbitrary"),
        name="sb_prompt",
    )(qsrc, kv, kv)


def _sbs_kernel(qbd_ref, kn_ref, vn_ref, kc_ref, vc_ref, o_ref, acc_s, car_s, *, tsub, n_sub, tnew):
    j = pl.program_id(1)
    qbd = qbd_ref[0]
    width = qbd.shape[1]

    def block(k, v, tk, mask):
        r = lax.broadcasted_iota(I32, (tk, tk), 0)
        c = lax.broadcasted_iota(I32, (tk, tk), 1)
        tri = jnp.where(c >= r, 1.0, 0.0).astype(BF16)
        z = _dot(k.astype(BF16), qbd)
        l1m = -_softplus(z)
        if mask is not None:
            l1m = jnp.where(mask, l1m, 0.0)
        cin = _dot(tri, l1m.astype(BF16))
        a = jnp.exp(z + car_s[...] + cin)
        if mask is not None:
            a = jnp.where(mask, a, 0.0)
        car_s[...] += cin[0:1, :]
        acc_s[...] += _dot_tn(a.astype(BF16), v.astype(BF16))

    @pl.when(j == 0)
    def _():
        acc_s[...] = jnp.zeros(acc_s.shape, F32)
        car_s[...] = jnp.zeros(car_s.shape, F32)
        s_idx = lax.broadcasted_iota(I32, (tnew, width), 0)
        t_idx = lax.broadcasted_iota(I32, (tnew, width), 1) % tnew
        block(kn_ref[0], vn_ref[0], tnew, s_idx < t_idx)

    for sub in range(n_sub - 1, -1, -1):
        rows = slice(sub * tsub, (sub + 1) * tsub)
        block(kc_ref[0, rows, :], vc_ref[0, rows, :], tsub, None)

    @pl.when(j == pl.num_programs(1) - 1)
    def _():
        heads = width // tnew
        acc = acc_s[...].reshape(heads, tnew, acc_s.shape[1])
        hrow = lax.broadcasted_iota(I32, acc.shape, 0)
        hcol = lax.broadcasted_iota(I32, acc.shape, 2) // B_HEAD_DIM
        o_ref[0] = jnp.sum(jnp.where(hrow == hcol, acc, 0.0), axis=0)


def sb_sample(qbd, k_new, v_new, k_cache, v_cache, *, tblk, tsub):
    batch, past, width = k_cache.shape
    tnew = k_new.shape[1]
    nblk = past // tblk
    return pl.pallas_call(
        functools.partial(_sbs_kernel, tsub=tsub, n_sub=tblk // tsub, tnew=tnew),
        grid=(batch, nblk),
        in_specs=[pl.BlockSpec((1,) + qbd.shape[1:], lambda b, j: (b, 0, 0)),
                  pl.BlockSpec((1, tnew, width), lambda b, j: (b, 0, 0)),
                  pl.BlockSpec((1, tnew, width), lambda b, j: (b, 0, 0)),
                  pl.BlockSpec((1, tblk, width), lambda b, j: (b, nblk - 1 - j, 0)),
                  pl.BlockSpec((1, tblk, width), lambda b, j: (b, nblk - 1 - j, 0))],
        out_specs=pl.BlockSpec((1, tnew, width), lambda b, j: (b, 0, 0)),
        out_shape=jax.ShapeDtypeStruct((batch, tnew, width), F32),
        scratch_shapes=[pltpu.VMEM((qbd.shape[2], width), F32), pltpu.VMEM((1, qbd.shape[2]), F32)],
        compiler_params=_cparams("parallel", "arbitrary"),
        name="sb_sample",
    )(qbd, k_new, v_new, k_cache, v_cache)


def _layer_norm(x, g, b):
    mu = jnp.mean(x, axis=1, keepdims=True)
    xc = x - mu
    var = jnp.mean(xc * xc, axis=1, keepdims=True)
    return xc * lax.rsqrt(var + LN_EPS) * g + b


def _tail_kernel(*refs, n_main, head_major, prompt_tiles):
    x_ref = refs[0]
    mainp_refs = refs[1:1 + n_main]
    mains_refs = refs[1 + n_main:1 + 2 * n_main]
    rest = refs[1 + 2 * n_main:]
    hmp_ref, hms_ref, wo_ref, wom_ref, g_ref, b_ref, rwh_ref, rwl_ref, rb_ref = rest[:9]
    x1_ref, x1b_ref, meta_ref, metat_ref, stat_ref = rest[9:]
    t = x_ref.shape[0]
    is_sample = pl.program_id(0) >= prompt_tiles

    def pick(p_ref, s_ref):
        if head_major:
            return jnp.where(is_sample, s_ref[0], p_ref[0]).astype(BF16)
        return jnp.where(is_sample, s_ref[...], p_ref[...]).astype(BF16)

    a = _dot(jnp.where(is_sample, hms_ref[...], hmp_ref[...]).astype(BF16), wom_ref[...])
    if head_major:
        for h in range(n_main):
            a += _dot(pick(mainp_refs[h], mains_refs[h]), wo_ref[h])
    else:
        a += _dot(pick(mainp_refs[0], mains_refs[0]), wo_ref[...])
    x1 = _layer_norm(DN_ALPHA * x_ref[...] + a, g_ref[...], b_ref[...])
    x1_ref[...] = x1
    x1b_ref[...] = x1.astype(BF16)

    xh, xl = _split_bf16(x1)
    logits = _dot(xh, rwh_ref[...]) + _dot(xl, rwh_ref[...]) + _dot(xh, rwl_ref[...]) + rb_ref[...]
    lane = lax.broadcasted_iota(I32, (t, LANES), 1)
    logits = jnp.where(lane < N_EXPERTS, logits, -jnp.inf)
    lane_f = lane.astype(F32)
    vals, sels = [], []
    for _ in range(TOP_K):
        mx = jnp.max(logits, axis=1, keepdims=True)
        idx = jnp.min(jnp.where(logits == mx, lane_f, float(LANES)), axis=1, keepdims=True)
        sel = lane_f == idx
        vals.append(mx)
        sels.append(sel)
        logits = jnp.where(sel, -jnp.inf, logits)
    exps = [jnp.exp(v - vals[0]) for v in vals]
    tot = exps[0] + exps[1] + exps[2] + exps[3]
    wts = [e / tot for e in exps]

    onehot = jnp.zeros((t, LANES), F32)
    for sel in sels:
        onehot = jnp.where(sel, 1.0, onehot)
    cnt = jnp.sum(onehot, axis=0, keepdims=True)
    pc = jnp.floor((cnt + (PACK_ROWS - 1)) * (1.0 / PACK_ROWS))
    er = lax.broadcasted_iota(I32, (LANES, LANES), 0)
    ec = lax.broadcasted_iota(I32, (LANES, LANES), 1)
    before = jnp.where(er < ec, 1.0, 0.0).astype(BF16)
    off = _dot(jnp.broadcast_to(pc, (8, LANES)).astype(BF16), before)[0:1, :]
    tr = lax.broadcasted_iota(I32, (t, t), 0)
    tc = lax.broadcasted_iota(I32, (t, t), 1)
    earlier = jnp.where(tc < tr, 1.0, 0.0).astype(BF16)
    rank = _dot(earlier, onehot.astype(BF16))
    posfull = off * float(PACK_ROWS) + rank
    meta = jnp.zeros((t, LANES), F32)
    for k in range(TOP_K):
        pos_k = jnp.sum(jnp.where(sels[k], posfull, 0.0), axis=1, keepdims=True)
        meta = jnp.where(lane == k, pos_k, meta)
        meta = jnp.where(lane == TOP_K + k, wts[k], meta)
    meta_ref[...] = meta
    metat_ref[0] = meta.T[0:8, :]
    srow = lax.broadcasted_iota(I32, (8, LANES), 0)
    stat_ref[0] = jnp.where(srow == 0, pc, jnp.where(srow == 1, off, 0.0))


def mixer_tail(x, main_p, main_s, hm_p, hm_s, wo_main, wo_mem, ln_g, ln_b, rwh, rwl, rb, *, head_major):
    n_tok, d = x.shape
    t = TOK_TILE
    nt = n_tok // t
    assert hm_s.shape[0] == t and hm_p.shape[0] == (nt - 1) * t
    last_p = nt - 2
    if head_major:
        n_main, _, hd = main_p.shape
        mainp_specs = [pl.BlockSpec((1, t, hd), functools.partial(lambda h, i: (h, jnp.minimum(i, last_p), 0), h))
                       for h in range(n_main)]
        mains_specs = [pl.BlockSpec((1, t, hd), functools.partial(lambda h, i: (h, 0, 0), h))
                       for h in range(n_main)]
        wo_spec = pl.BlockSpec(wo_main.shape, lambda i: (0, 0, 0))
    else:
        n_main = 1
        mainp_specs = [pl.BlockSpec((t, MAIN_WIDTH), lambda i: (jnp.minimum(i, last_p), 0))]
        mains_specs = [pl.BlockSpec((t, MAIN_WIDTH), lambda i: (0, 0))]
        wo_spec = pl.BlockSpec(wo_main.shape, lambda i: (0, 0))
    const2 = lambda i: (0, 0)
    return pl.pallas_call(
        functools.partial(_tail_kernel, n_main=n_main, head_major=head_major, prompt_tiles=nt - 1),
        grid=(nt,),
        in_specs=[pl.BlockSpec((t, d), lambda i: (i, 0))] + mainp_specs + mains_specs + [
            pl.BlockSpec((t, MEM_WIDTH), lambda i: (jnp.minimum(i, last_p), 0)),
            pl.BlockSpec((t, MEM_WIDTH), const2),
            wo_spec,
            pl.BlockSpec(wo_mem.shape, const2),
            pl.BlockSpec((1, d), const2), pl.BlockSpec((1, d), const2),
            pl.BlockSpec((d, LANES), const2), pl.BlockSpec((d, LANES), const2),
            pl.BlockSpec((1, LANES), const2)],
        out_specs=[pl.BlockSpec((t, d), lambda i: (i, 0)),
                   pl.BlockSpec((t, d), lambda i: (i, 0)),
                   pl.BlockSpec((t, LANES), lambda i: (i, 0)),
                   pl.BlockSpec((1, 8, t), lambda i: (i, 0, 0)),
                   pl.BlockSpec((1, 8, LANES), lambda i: (i, 0, 0))],
        out_shape=[jax.ShapeDtypeStruct((n_tok, d), F32),
                   jax.ShapeDtypeStruct((n_tok, d), BF16),
                   jax.ShapeDtypeStruct((n_tok, LANES), F32),
                   jax.ShapeDtypeStruct((nt, 8, t), F32),
                   jax.ShapeDtypeStruct((nt, 8, LANES), F32)],
        compiler_params=_cparams("parallel"),
        name="mixer_tail",
    )(x, *([main_p] * n_main), *([main_s] * n_main), hm_p, hm_s, wo_main, wo_mem, ln_g, ln_b, rwh, rwl, rb)


def _tile_rows():
    need = TOK_TILE * TOP_K + N_EXPERTS * (PACK_ROWS - 1)
    return -(-need // 256) * 256


def _piece_copies(dst_ref, tile, make_copy, action, first=0, last=None):
    pieces = _tile_rows() // PACK_ROWS
    for p in range(first, pieces if last is None else last):
        g = pl.multiple_of(dst_ref[tile * pieces + p] * PACK_ROWS, PACK_ROWS)
        action(make_copy(p * PACK_ROWS, g))


def _zero_fill(ts_ref, tn_ref, nu_ref, xg_ref, zero_s, sem, action):
    def per_expert(e, _):
        def per_piece(p, _):
            g = pl.multiple_of((ts_ref[e] + p) * PACK_ROWS, PACK_ROWS)
            action(pltpu.make_async_copy(zero_s.at[pl.ds(0, PACK_ROWS)], xg_ref.at[pl.ds(g, PACK_ROWS)], sem))
            return 0

        lax.fori_loop(0, tn_ref[e], per_piece, 0)
        return 0

    lax.fori_loop(0, N_EXPERTS, per_expert, 0)

    def per_tile(i, _):
        g = pl.multiple_of(i * EXP_TILE, EXP_TILE)
        action(pltpu.make_async_copy(zero_s, xg_ref.at[pl.ds(g, EXP_TILE)], sem))
        return 0

    lax.fori_loop(nu_ref[0], xg_ref.shape[0] // EXP_TILE, per_tile, 0)


def _dispatch_kernel(dst_ref, ts_ref, tn_ref, nu_ref, metat_ref, x_ref, xg_ref,
                     xs_s, zero_s, sem, zsem):
    tile = pl.program_id(0)
    rt, t = xs_s.shape[1], x_ref.shape[0]
    buf = tile % 2

    @pl.when(tile == 0)
    def _():
        zero_s[...] = jnp.zeros(zero_s.shape, zero_s.dtype)
        _zero_fill(ts_ref, tn_ref, nu_ref, xg_ref, zero_s, zsem, lambda cp: cp.start())
        _zero_fill(ts_ref, tn_ref, nu_ref, xg_ref, zero_s, zsem, lambda cp: cp.wait())

    def copies_of(b):
        def make_copy(s, g):
            return pltpu.make_async_copy(xs_s.at[b, pl.ds(s, PACK_ROWS)], xg_ref.at[pl.ds(g, PACK_ROWS)],
                                         sem.at[b])
        return make_copy

    posr = metat_ref[0]
    x = x_ref[...]
    per_chunk = MOE_CHUNK // PACK_ROWS
    for c in range(rt // MOE_CHUNK):
        slot = (lax.broadcasted_iota(I32, (MOE_CHUNK, t), 0) + c * MOE_CHUNK).astype(F32)
        hit = slot == posr[0:1, :]
        for k in range(1, TOP_K):
            hit = hit | (slot == posr[k:k + 1, :])
        rows = slice(c * MOE_CHUNK, (c + 1) * MOE_CHUNK)
        xs_s[buf, rows, :] = _dot(jnp.where(hit, 1.0, 0.0).astype(BF16), x).astype(BF16)
        _piece_copies(dst_ref, tile, copies_of(buf), lambda cp: cp.start(),
                      c * per_chunk, (c + 1) * per_chunk)

    @pl.when(tile > 0)
    def _():
        _piece_copies(dst_ref, tile - 1, copies_of(1 - buf), lambda cp: cp.wait())

    @pl.when(tile == pl.num_programs(0) - 1)
    def _():
        _piece_copies(dst_ref, tile, copies_of(buf), lambda cp: cp.wait())


def moe_dispatch(dstp, tail_start, tail_n, nused, metat, x1b, n_rows):
    n_tok, d = x1b.shape
    t = TOK_TILE
    rt = _tile_rows()
    return pl.pallas_call(
        _dispatch_kernel,
        grid_spec=pltpu.PrefetchScalarGridSpec(
            num_scalar_prefetch=4,
            grid=(n_tok // t,),
            in_specs=[pl.BlockSpec((1, 8, t), lambda i, *_: (i, 0, 0)),
                      pl.BlockSpec((t, d), lambda i, *_: (i, 0))],
            out_specs=pl.BlockSpec(memory_space=pl.ANY),
            scratch_shapes=[pltpu.VMEM((2, rt, d), BF16), pltpu.VMEM((EXP_TILE, d), BF16),
                            pltpu.SemaphoreType.DMA((2,)), pltpu.SemaphoreType.DMA(())]),
        out_shape=jax.ShapeDtypeStruct((n_rows + 2 * rt, d), BF16),
        compiler_params=_cparams("arbitrary"),
        name="moe_dispatch",
    )(dstp, tail_start, tail_n, nused, metat, x1b)


def _expert_kernel(te_ref, nu_ref, x_ref, wgu_ref, bgu_ref, wd_ref, bd_ref, y_ref, wgu_s, wd_s):
    i = pl.program_id(0)
    prev = te_ref[jnp.maximum(i - 1, 0)]

    @pl.when((i == 0) | (te_ref[i] != prev))
    def _():
        wgu_s[...] = wgu_ref[0].astype(BF16)
        wd_s[...] = wd_ref[0].astype(BF16)

    @pl.when(i < nu_ref[0])
    def _():
        x = x_ref[...]
        f = wd_s.shape[0]
        half = f // 2
        y = jnp.broadcast_to(bd_ref[0], y_ref.shape)
        for c in range(2):
            cols = slice(c * half, (c + 1) * half)
            ucols = slice(f + c * half, f + (c + 1) * half)
            gate = jnp.minimum(_dot(x, wgu_s[:, cols]) + bgu_ref[0, :, cols], SWIGLU_LIMIT)
            up = jnp.clip(_dot(x, wgu_s[:, ucols]) + bgu_ref[0, :, ucols], -SWIGLU_LIMIT, SWIGLU_LIMIT)
            act = (up + 1.0) * gate * jax.nn.sigmoid(SWIGLU_ALPHA * gate)
            y = y + _dot(act.astype(BF16), wd_s[cols, :])
        y_ref[...] = y.astype(y_ref.dtype)

    @pl.when(i >= nu_ref[0])
    def _():
        y_ref[...] = jnp.zeros(y_ref.shape, y_ref.dtype)


def moe_experts(te, nused, xg, wgu, bgu, wd, bd, n_rows):
    d = xg.shape[1]
    f2 = wgu.shape[-1]
    tm = EXP_TILE

    def row_map(i, te_ref, nu_ref):
        return (jnp.minimum(i, nu_ref[0] - 1), 0)

    def w_map(i, te_ref, nu_ref):
        return (te_ref[i], 0, 0)

    return pl.pallas_call(
        _expert_kernel,
        grid_spec=pltpu.PrefetchScalarGridSpec(
            num_scalar_prefetch=2,
            grid=(n_rows // tm,),
            in_specs=[pl.BlockSpec((tm, d), row_map),
                      pl.BlockSpec((1, d, f2), w_map),
                      pl.BlockSpec((1, 1, f2), w_map),
                      pl.BlockSpec((1, f2 // 2, d), w_map),
                      pl.BlockSpec((1, 1, d), w_map)],
            out_specs=pl.BlockSpec((tm, d), lambda i, *_: (i, 0)),
            scratch_shapes=[pltpu.VMEM((d, f2), BF16), pltpu.VMEM((f2 // 2, d), BF16)]),
        out_shape=jax.ShapeDtypeStruct((n_rows, d), BF16),
        compiler_params=_cparams("arbitrary"),
        name="moe_experts",
    )(te, nused, xg, wgu, bgu, wd, bd)


def _combine_kernel(dst_ref, meta_ref, x1_ref, g_ref, b_ref, yg_ref, x2_ref, ys_s, sem):
    tile = pl.program_id(0)
    last = pl.num_programs(0) - 1
    rt, t = ys_s.shape[1], x1_ref.shape[0]
    buf = tile % 2

    def copies_of(b):
        def make_copy(s, g):
            return pltpu.make_async_copy(yg_ref.at[pl.ds(g, PACK_ROWS)], ys_s.at[b, pl.ds(s, PACK_ROWS)],
                                         sem.at[b])
        return make_copy

    @pl.when(tile == 0)
    def _():
        _piece_copies(dst_ref, tile, copies_of(buf), lambda cp: cp.start())

    _piece_copies(dst_ref, tile, copies_of(buf), lambda cp: cp.wait())
    nxt = jnp.minimum(tile + 1, last)
    meta = meta_ref[...]
    f = jnp.zeros(x1_ref.shape, F32)
    per_chunk = MOE_CHUNK // PACK_ROWS
    for c in range(rt // MOE_CHUNK):
        _piece_copies(dst_ref, nxt, copies_of(1 - buf), lambda cp: cp.start(),
                      c * per_chunk, (c + 1) * per_chunk)
        slot = (lax.broadcasted_iota(I32, (t, MOE_CHUNK), 1) + c * MOE_CHUNK).astype(F32)
        wmat = jnp.zeros((t, MOE_CHUNK), F32)
        for k in range(TOP_K):
            wmat = jnp.where(slot == meta[:, k:k + 1], meta[:, TOP_K + k:TOP_K + k + 1], wmat)
        f += _dot(wmat.astype(BF16), ys_s[buf, c * MOE_CHUNK:(c + 1) * MOE_CHUNK, :])
    x2_ref[...] = _layer_norm(DN_ALPHA * x1_ref[...] + f, g_ref[...], b_ref[...])

    @pl.when(tile == last)
    def _():
        _piece_copies(dst_ref, tile, copies_of(1 - buf), lambda cp: cp.wait())


def moe_combine(dstp, meta, x1, ln_g, ln_b, yg):
    n_tok, d = x1.shape
    t = TOK_TILE
    rt = _tile_rows()
    return pl.pallas_call(
        _combine_kernel,
        grid_spec=pltpu.PrefetchScalarGridSpec(
            num_scalar_prefetch=1,
            grid=(n_tok // t,),
            in_specs=[pl.BlockSpec((t, LANES), lambda i, *_: (i, 0)),
                      pl.BlockSpec((t, d), lambda i, *_: (i, 0)),
                      pl.BlockSpec((1, d), lambda i, *_: (0, 0)),
                      pl.BlockSpec((1, d), lambda i, *_: (0, 0)),
                      pl.BlockSpec(memory_space=pl.ANY)],
            out_specs=pl.BlockSpec((t, d), lambda i, *_: (i, 0)),
            scratch_shapes=[pltpu.VMEM((2, rt, d), BF16), pltpu.SemaphoreType.DMA((2,))]),
        out_shape=jax.ShapeDtypeStruct((n_tok, d), F32),
        compiler_params=_cparams("arbitrary"),
        name="moe_combine",
    )(dstp, meta, x1, ln_g, ln_b, yg)


def _route_tables(stats, n_rows):
    pc = stats[:, 0, :N_EXPERTS].astype(I32)
    off = stats[:, 1, :N_EXPERTS].astype(I32)
    pieces_e = jnp.sum(pc, axis=0)
    per_tile = EXP_TILE // PACK_ROWS
    tiles_e = (pieces_e + per_tile - 1) // per_tile
    ends = jnp.cumsum(tiles_e)
    base_e = (ends - tiles_e) * per_tile
    gd = base_e[None, :] + jnp.cumsum(pc, axis=0) - pc
    tail_start = base_e + pieces_e
    tail_n = tiles_e * per_tile - pieces_e
    nused = ends[-1]
    tile_ids = jnp.minimum(jnp.arange(n_rows // EXP_TILE, dtype=I32), nused - 1)
    te = jnp.sum((ends[None, :] <= tile_ids[:, None]).astype(I32), axis=1)
    seg_end = off + pc
    pidx = jnp.arange(_tile_rows() // PACK_ROWS, dtype=I32)
    e_of_p = jnp.sum((seg_end[:, None, :] <= pidx[None, :, None]).astype(I32), axis=2)
    sel = e_of_p[:, :, None] == jnp.arange(N_EXPERTS, dtype=I32)[None, None, :]
    dstp = jnp.sum(jnp.where(sel, (gd - off)[:, None, :], 0), axis=2) + pidx[None, :]
    used = pidx[None, :] < seg_end[:, -1:]
    spare = (n_rows // PACK_ROWS + (jnp.arange(pc.shape[0], dtype=I32) % 2)[:, None] * pidx.shape[0]
             + pidx[None, :])
    dst_disp = jnp.where(used, dstp, spare)
    dst_comb = jnp.where(used, dstp, 0)
    return (dst_disp.reshape(-1).astype(I32), dst_comb.reshape(-1).astype(I32), tail_start.astype(I32),
            tail_n.astype(I32), te, nused.reshape(1).astype(I32))


def moe_layer(x1, x1b, meta, metat, stats, ln_g, ln_b, wgu, bgu, wd, bd, layer):
    n_tok = x1.shape[0]
    nt = n_tok // TOK_TILE
    worst = n_tok * TOP_K + nt * N_EXPERTS * (PACK_ROWS - 1) + N_EXPERTS * (EXP_TILE - 1)
    n_rows = -(-worst // EXP_TILE) * EXP_TILE
    dst_disp, dst_comb, tail_start, tail_n, te, nused = _route_tables(stats, n_rows)
    xg = moe_dispatch(dst_disp, tail_start, tail_n, nused, metat, x1b, n_rows)
    yg = moe_experts(te + layer * N_EXPERTS, nused, xg, wgu, bgu, wd, bd, n_rows)
    return moe_combine(dst_comb, meta, x1, ln_g, ln_b, yg)


def _pad_cols(w, width):
    return jnp.pad(w, ((0, 0), (0, width - w.shape[1])))


def _hi_lo(w):
    hi = w.astype(BF16)
    return hi, (w - hi.astype(F32)).astype(BF16)


def kernel(x_prompt, x_sample, state_mlstm_C, state_mlstm_n, state_mlstm_m, cache_sb_k, cache_sb_v,
           cache_mem_k, cache_mem_v, mem_prompt, w_in_a, b_gate_a, w_in_b, w_kv_b, w_mem_kv, w_out,
           ln_g, ln_b, router_w, router_b, w_gate_up, b_gate_up, w_down, b_down):
    bp, seq, d = x_prompt.shape
    bs, dseq, _ = x_sample.shape
    n_mem = mem_prompt.shape[1]
    past = cache_sb_k.shape[1]
    n_p = bp * seq
    n_s = bs * dseq
    hd = A_HEAD_DIM
    x = jnp.concatenate([x_prompt.reshape(n_p, d), x_sample.reshape(n_s, d)], axis=0)
    n_tok = n_p + n_s
    tm = n_tok // 26 if n_tok % 26 == 0 and (n_tok // 26) % 8 == 0 else TOK_TILE

    wm = jnp.transpose(w_mem_kv, (1, 0, 2)).reshape(d, DEPTH * 2 * MEM_WIDTH).astype(BF16)
    (mkv,) = matmul(mem_prompt.reshape(bp * n_mem, d), wm, tm=512, tn=512)
    mkv = mkv.reshape(bp, n_mem, DEPTH, 2, MEM_WIDTH)
    p_mem_k = jnp.transpose(mkv[:, :, :, 0], (2, 0, 1, 3))
    p_mem_v = jnp.transpose(mkv[:, :, :, 1], (2, 0, 1, 3))
    s_mem_k = cache_mem_k.reshape(DEPTH, bs, n_mem, MEM_WIDTH)
    s_mem_v = cache_mem_v.reshape(DEPTH, bs, n_mem, MEM_WIDTH)

    def pack_nm(n, m):
        b = n.shape[0]
        out = jnp.zeros((b, A_HEADS, 8, 2 * LANES), F32)
        out = out.at[:, :, 0, :hd].set(n)
        return out.at[:, :, 1, :].set(jnp.broadcast_to(m[:, :, None], (b, A_HEADS, 2 * LANES)))

    k_cache = cache_sb_k.reshape(bs, past, MAIN_WIDTH)
    v_cache = cache_sb_v.reshape(bs, past, MAIN_WIDTH)
    n_exp = DEPTH * N_EXPERTS
    wgu_all = w_gate_up.reshape(n_exp, d, 2 * D_FF)
    bgu_all = b_gate_up.reshape(n_exp, 1, 2 * D_FF)
    wd_all = w_down.reshape(n_exp, D_FF, d)
    bd_all = b_down.reshape(n_exp, 1, d)

    cs, ns, ms = [], [], []
    k_new = v_new = kv16 = None
    for l in range(DEPTH):
        rwh, rwl = _hi_lo(_pad_cols(router_w[l], LANES))
        rb = _pad_cols(router_b[l][None, :], LANES)
        wo = w_out[l].astype(BF16)
        if l < N_A:
            w = w_in_a[l]
            wg = jnp.transpose(w[:, :4 * MAIN_WIDTH].reshape(d, 4 * A_HEADS, hd), (1, 0, 2)).astype(BF16)
            wgh, wgl = _hi_lo(_pad_cols(w[:, 4 * MAIN_WIDTH:4 * MAIN_WIDTH + 2 * A_HEADS], LANES))
            wqm = w[:, 4 * MAIN_WIDTH + 2 * A_HEADS:].astype(BF16)
            p, qm, gates = proj_a(x, wg, wqm, wgh, wgl, tm=tm)
            gates = gates[:, :2 * A_HEADS] + b_gate_a[l][None, :]
            g_p = jnp.transpose(gates[:n_p].reshape(bp, seq, 2 * A_HEADS), (0, 2, 1))
            g_s = jnp.transpose(gates[n_p:].reshape(bs, dseq, 2 * A_HEADS), (0, 2, 1))
            zc = jnp.zeros((bp, A_HEADS, hd, hd), F32)
            znm = jnp.zeros((bp, A_HEADS, 8, 2 * LANES), F32)
            h_p, c_p, nm_p = mlstm(p, g_p, zc, znm, batch=bp, seq=seq, row_base=0,
                                   rows_per_step=4 * MLSTM_CHUNK, chunk=MLSTM_CHUNK)
            h_s, c_s, nm_s = mlstm(p, g_s, state_mlstm_C[l], pack_nm(state_mlstm_n[l], state_mlstm_m[l]),
                                   batch=bs, seq=dseq, row_base=n_p, rows_per_step=dseq, chunk=dseq)
            cs.append((c_p, c_s))
            ns.append((nm_p[:, :, 0, :hd], nm_s[:, :, 0, :hd]))
            ms.append((nm_p[:, :, 1, 0], nm_s[:, :, 1, 0]))
            hm_p = mem_attend(qm, 0, p_mem_k[l], p_mem_v[l], batch=bp, seq=seq, row_base=0, tq=512)
            hm_s = mem_attend(qm, 0, s_mem_k[l], s_mem_v[l], batch=bs, seq=dseq, row_base=n_p, tq=dseq)
            wo_main = wo[:MAIN_WIDTH].reshape(A_HEADS, hd, d)
            head_major = True
        else:
            if l == N_A:
                kv32, kv16 = matmul(x, w_kv_b.astype(BF16), tm=tm, tn=512, out_dtypes=(F32, BF16))
                k_new = kv32[:, :MAIN_WIDTH]
                v_new = kv32[:, MAIN_WIDTH:]
            (pq,) = matmul(x, w_in_b[l - N_A].astype(BF16), tm=tm, tn=512)
            h_p = sb_prompt(pq, kv16, batch=bp, seq=seq, tq=256, heads=4)
            q_s = pq[n_p:, :MAIN_WIDTH].reshape(bs, dseq, B_HEADS, B_HEAD_DIM) * (B_HEAD_DIM ** -0.5)
            qbd = jnp.einsum("bthd,hg->bhdgt", q_s, jnp.eye(B_HEADS, dtype=F32))
            qbd = qbd.reshape(bs, MAIN_WIDTH, B_HEADS * dseq).astype(BF16)
            h_s = sb_sample(qbd, k_new[n_p:].reshape(bs, dseq, MAIN_WIDTH),
                            v_new[n_p:].reshape(bs, dseq, MAIN_WIDTH), k_cache, v_cache,
                            tblk=1024, tsub=256).reshape(n_s, MAIN_WIDTH)
            hm_p = mem_attend(pq, MAIN_WIDTH // MEM_WIDTH, p_mem_k[l], p_mem_v[l],
                              batch=bp, seq=seq, row_base=0, tq=512)
            hm_s = mem_attend(pq, MAIN_WIDTH // MEM_WIDTH, s_mem_k[l], s_mem_v[l],
                              batch=bs, seq=dseq, row_base=n_p, tq=dseq)
            wo_main = wo[:MAIN_WIDTH]
            head_major = False
        x1, x1b, meta, metat, stats = mixer_tail(
            x, h_p, h_s, hm_p, hm_s, wo_main, wo[MAIN_WIDTH:], ln_g[l, 0][None, :], ln_b[l, 0][None, :],
            rwh, rwl, rb, head_major=head_major)
        x = moe_layer(x1, x1b, meta, metat, stats, ln_g[l, 1][None, :], ln_b[l, 1][None, :],
                      wgu_all, bgu_all, wd_all, bd_all, l)

    y_prompt = x[:n_p].reshape(bp, seq, d)
    y_sample = x[n_p:].reshape(bs, dseq, d)
    p_c = jnp.stack([c[0] for c in cs])
    s_c = jnp.stack([c[1] for c in cs])
    p_n = jnp.stack([n[0] for n in ns])
    s_n = jnp.stack([n[1] for n in ns])
    p_m = jnp.stack([m[0] for m in ms])
    s_m = jnp.stack([m[1] for m in ms])
    p_sb_k = k_new[:n_p].reshape(bp, seq, B_HEADS, B_HEAD_DIM)
    p_sb_v = v_new[:n_p].reshape(bp, seq, B_HEADS, B_HEAD_DIM)
    s_sb_k = k_new[n_p:].reshape(bs, dseq, B_HEADS, B_HEAD_DIM)
    s_sb_v = v_new[n_p:].reshape(bs, dseq, B_HEADS, B_HEAD_DIM)
    pmk = p_mem_k.reshape(DEPTH, bp, n_mem, MEM_HEADS, MEM_HEAD_DIM)
    pmv = p_mem_v.reshape(DEPTH, bp, n_mem, MEM_HEADS, MEM_HEAD_DIM)
    return (y_prompt, y_sample, p_c, p_n, p_m, p_sb_k, p_sb_v, pmk, pmv, s_c, s_n, s_m, s_sb_k, s_sb_v)
```

```python
import functools

import jax
import jax.numpy as jnp
from jax import lax
from jax.experimental import pallas as pl
from jax.experimental.pallas import tpu as pltpu

F32 = jnp.float32
BF16 = jnp.bfloat16
I32 = jnp.int32

D_MODEL = 1024
DEPTH = 4
N_A = DEPTH // 2
MEM_WIDTH = D_MODEL // 4
MAIN_WIDTH = D_MODEL - MEM_WIDTH
MEM_HEADS = 4
MEM_HEAD_DIM = MEM_WIDTH // MEM_HEADS
A_HEADS = 4
A_HEAD_DIM = MAIN_WIDTH // A_HEADS
B_HEADS = 12
B_HEAD_DIM = MAIN_WIDTH // B_HEADS
N_EXPERTS = 32
TOP_K = 4
D_FF = D_MODEL
SWIGLU_LIMIT = 7.0
SWIGLU_ALPHA = 1.702
DN_ALPHA = (2.0 * DEPTH) ** 0.25
LN_EPS = 1e-5
LOG2E = 1.4426950408889634

LANES = 128
PACK_ROWS = 16
TOK_TILE = 512
EXP_TILE = 512
MOE_CHUNK = 512
MLSTM_CHUNK = 256
VMEM_LIMIT = 56 * 1024 * 1024


def _cparams(*sem):
    return pltpu.CompilerParams(dimension_semantics=sem, vmem_limit_bytes=VMEM_LIMIT)


def _dot(a, b):
    return jnp.dot(a, b, preferred_element_type=F32)


def _dot_nt(a, b):
    return lax.dot_general(a, b, (((1,), (1,)), ((), ())), preferred_element_type=F32)


def _dot_tn(a, b):
    return lax.dot_general(a, b, (((0,), (0,)), ((), ())), preferred_element_type=F32)


def _split_bf16(x):
    hi = x.astype(BF16)
    lo = (x - hi.astype(F32)).astype(BF16)
    return hi, lo


def _softplus(x):
    return jnp.maximum(x, 0.0) + jnp.log(1.0 + jnp.exp(-jnp.abs(x)))


def _neg_abs(x):
    bits = lax.bitcast_convert_type(x, jnp.uint32) | jnp.uint32(0x80000000)
    return lax.bitcast_convert_type(bits, F32)


def _mm_kernel(x_ref, w_ref, *o_refs):
    y = _dot(x_ref[...].astype(BF16), w_ref[...])
    for o_ref in o_refs:
        o_ref[...] = y.astype(o_ref.dtype)


def matmul(x, w, *, tm, tn, out_dtypes=(F32,)):
    m, k = x.shape
    n = w.shape[1]
    outs = pl.pallas_call(
        _mm_kernel,
        grid=(m // tm, n // tn),
        in_specs=[pl.BlockSpec((tm, k), lambda i, j: (i, 0)),
                  pl.BlockSpec((k, tn), lambda i, j: (0, j))],
        out_specs=[pl.BlockSpec((tm, tn), lambda i, j: (i, j)) for _ in out_dtypes],
        out_shape=[jax.ShapeDtypeStruct((m, n), dt) for dt in out_dtypes],
        compiler_params=_cparams("parallel", "arbitrary"),
        name="matmul",
    )(x, w)
    return outs


def _proj_a_kernel(x_ref, wg_ref, wqm_ref, wg2_ref, p_ref, qm_ref, gate_ref):
    x = x_ref[...]
    xh, xl = _split_bf16(x)
    for g in range(wg_ref.shape[0]):
        p_ref[g] = _dot(xh, wg_ref[g])

    @pl.when(pl.program_id(1) == 0)
    def _():
        qm_ref[...] = _dot(xh, wqm_ref[...])
        both = _dot(xh, wg2_ref[...])
        gate_ref[...] = both[:, :LANES] + both[:, LANES:] + _dot(xl, wg2_ref[:, :LANES])


def proj_a(x, wg, wqm, wg2, *, tm):
    m, k = x.shape
    g, _, hd = wg.shape
    gs = A_HEADS
    return pl.pallas_call(
        _proj_a_kernel,
        grid=(m // tm, g // gs),
        in_specs=[pl.BlockSpec((tm, k), lambda i, j: (i, 0)),
                  pl.BlockSpec((gs, k, hd), lambda i, j: (j, 0, 0)),
                  pl.BlockSpec((k, MEM_WIDTH), lambda i, j: (0, 0)),
                  pl.BlockSpec((k, 2 * LANES), lambda i, j: (0, 0))],
        out_specs=[pl.BlockSpec((gs, tm, hd), lambda i, j: (j, i, 0)),
                   pl.BlockSpec((tm, MEM_WIDTH), lambda i, j: (i, 0)),
                   pl.BlockSpec((tm, LANES), lambda i, j: (i, 0))],
        out_shape=[jax.ShapeDtypeStruct((g, m, hd), F32),
                   jax.ShapeDtypeStruct((m, MEM_WIDTH), F32),
                   jax.ShapeDtypeStruct((m, LANES), F32)],
        compiler_params=_cparams("parallel", "arbitrary"),
        name="proj_a",
    )(x, wg, wqm, wg2)


def _mlstm_kernel(q_ref, k_ref, v_ref, o_ref, g_ref, c0_ref, nm0_ref,
                  h_ref, cn_ref, nmn_ref, c_s, nm_s, *, chunk, n_chunks, heads, hd):
    step = pl.program_id(1)
    L = chunk

    @pl.when(step == 0)
    def _():
        c_s[...] = c0_ref[0]
        nm_s[...] = nm0_ref[0]

    row = lax.broadcasted_iota(I32, (L, L), 0)
    col = lax.broadcasted_iota(I32, (L, L), 1)
    eye = row == col
    tril = col <= row
    scale = hd ** -0.5

    for h in range(heads):
        local = []
        for ch in range(n_chunks):
            rows = slice(ch * L, (ch + 1) * L)
            q = q_ref[h, rows, :]
            k = k_ref[h, rows, :] * scale
            v = v_ref[h, rows, :]
            li = g_ref[0, h:h + 1, rows]
            lf = -_softplus(-g_ref[0, heads + h:heads + h + 1, rows])
            lf_b = jnp.broadcast_to(lf, (L, L))
            b_col = jnp.sum(jnp.where(tril, lf_b, 0.0), axis=1, keepdims=True)
            lf_col = jnp.sum(jnp.where(eye, lf_b, 0.0), axis=1, keepdims=True)
            b_row = jnp.sum(jnp.where(row <= col, jnp.broadcast_to(lf_col, (L, L)), 0.0),
                            axis=0, keepdims=True)
            d = jnp.where(tril, b_col - b_row + li, -jnp.inf)
            m_loc = jnp.max(d, axis=1, keepdims=True)
            qb = q.astype(BF16)
            vb = v.astype(BF16)
            s = _dot_nt(qb, k.astype(BF16)) * jnp.exp(d - m_loc)
            sv = _dot(s.astype(BF16), vb)
            rs = jnp.sum(s, axis=1, keepdims=True)
            b_last = b_col[L - 1:L, :]
            dec = b_last - b_row + li
            m_dec = jnp.max(dec, axis=1, keepdims=True)
            w_row = jnp.exp(dec - m_dec)
            w_col = jnp.sum(jnp.where(eye, jnp.broadcast_to(w_row, (L, L)), 0.0), axis=1, keepdims=True)
            kw = k * w_col
            kv = _dot_tn(kw.astype(BF16), vb)
            ks = jnp.sum(kw, axis=0, keepdims=True)
            local.append((rows, q, qb, b_col, m_loc, sv, rs, b_last, m_dec, kv, ks))

        m_prev = nm_s[h, 1:2, 0:1]
        n_row = nm_s[h, 0:1, 0:hd]
        c_old = c_s[h]
        for rows, q, qb, b_col, m_loc, sv, rs, b_last, m_dec, kv, ks in local:
            inter = b_col + m_prev
            m_t = jnp.maximum(inter, m_loc)
            g = jnp.exp(inter - m_t)
            f = jnp.exp(m_loc - m_t)
            num = g * _dot(qb, c_old.astype(BF16)) + f * sv
            den = g * jnp.sum(q * n_row, axis=1, keepdims=True) + f * rs
            hh = num / jnp.maximum(jnp.abs(den), jnp.exp(-m_t))
            h_ref[h, rows, :] = hh * jax.nn.sigmoid(o_ref[h, rows, :])
            m_new = jnp.maximum(b_last + m_prev, m_dec)
            g_c = jnp.exp(b_last + m_prev - m_new)
            f_c = jnp.exp(m_dec - m_new)
            c_old = g_c * c_old + f_c * kv
            n_row = g_c * n_row + f_c * ks
            m_prev = m_new
        c_s[h] = c_old
        nm_s[h, 0:1, 0:hd] = n_row
        nm_s[h, 1:2, :] = jnp.broadcast_to(m_prev, (1, nm_s.shape[2]))

    @pl.when(step == pl.num_programs(1) - 1)
    def _():
        cn_ref[0] = c_s[...]
        nmn_ref[0] = nm_s[...]


def mlstm(p, gates, c0, nm0, *, batch, seq, row_base, rows_per_step, chunk):
    hd = p.shape[-1]
    heads = A_HEADS
    steps = seq // rows_per_step
    base = row_base // rows_per_step
    nmw = nm0.shape[-1]

    def tok_map(grp):
        return lambda b, c: (grp, base + b * steps + c, 0)

    kern = functools.partial(_mlstm_kernel, chunk=chunk, n_chunks=rows_per_step // chunk,
                             heads=heads, hd=hd)
    return pl.pallas_call(
        kern,
        grid=(batch, steps),
        in_specs=[pl.BlockSpec((heads, rows_per_step, hd), tok_map(0)),
                  pl.BlockSpec((heads, rows_per_step, hd), tok_map(1)),
                  pl.BlockSpec((heads, rows_per_step, hd), tok_map(2)),
                  pl.BlockSpec((heads, rows_per_step, hd), tok_map(3)),
                  pl.BlockSpec((1, 2 * heads, rows_per_step), lambda b, c: (b, 0, c)),
                  pl.BlockSpec((1, heads, hd, hd), lambda b, c: (b, 0, 0, 0)),
                  pl.BlockSpec((1, heads, 8, nmw), lambda b, c: (b, 0, 0, 0))],
        out_specs=[pl.BlockSpec((heads, rows_per_step, hd), lambda b, c: (0, b * steps + c, 0)),
                   pl.BlockSpec((1, heads, hd, hd), lambda b, c: (b, 0, 0, 0)),
                   pl.BlockSpec((1, heads, 8, nmw), lambda b, c: (b, 0, 0, 0))],
        out_shape=[jax.ShapeDtypeStruct((heads, batch * seq, hd), F32),
                   jax.ShapeDtypeStruct((batch, heads, hd, hd), F32),
                   jax.ShapeDtypeStruct((batch, heads, 8, nmw), F32)],
        scratch_shapes=[pltpu.VMEM((heads, hd, hd), F32), pltpu.VMEM((heads, 8, nmw), F32)],
        compiler_params=_cparams("parallel", "arbitrary"),
        name="mlstm",
    )(p, p, p, p, gates, c0, nm0)


def _mem_kernel(q_ref, mk_ref, mv_ref, o_ref):
    q = q_ref[...]
    mk = mk_ref[0].astype(BF16)
    mv = mv_ref[0].astype(BF16)
    lane = lax.broadcasted_iota(I32, q.shape, 1)
    out = jnp.zeros(q.shape, F32)
    for h in range(MEM_HEADS):
        sel = (lane >= h * MEM_HEAD_DIM) & (lane < (h + 1) * MEM_HEAD_DIM)
        qh = jnp.where(sel, q, 0.0).astype(BF16)
        s = _dot_nt(qh, mk) * (MEM_HEAD_DIM ** -0.5)
        s = s - jnp.max(s, axis=1, keepdims=True)
        e = jnp.exp(s)
        p = e / jnp.sum(e, axis=1, keepdims=True)
        out = jnp.where(sel, _dot(p.astype(BF16), mv), out)
    o_ref[...] = out


def mem_attend(qsrc, col_block, mk, mv, *, batch, seq, row_base, tq):
    steps = seq // tq
    base = row_base // tq
    return pl.pallas_call(
        _mem_kernel,
        grid=(batch, steps),
        in_specs=[pl.BlockSpec((tq, MEM_WIDTH), lambda b, i: (base + b * steps + i, col_block)),
                  pl.BlockSpec((1,) + mk.shape[1:], lambda b, i: (b, 0, 0)),
                  pl.BlockSpec((1,) + mv.shape[1:], lambda b, i: (b, 0, 0))],
        out_specs=pl.BlockSpec((tq, MEM_WIDTH), lambda b, i: (b * steps + i, 0)),
        out_shape=jax.ShapeDtypeStruct((batch * seq, MEM_WIDTH), F32),
        compiler_params=_cparams("parallel", "arbitrary"),
        name="mem_attend",
    )(qsrc, mk, mv)


def _sbp_kernel(q_ref, k_ref, v_ref, o_ref, w_s, tot_s, car_s, acc_s, *, tq, heads):
    qi = pl.program_id(2)
    qn = q_ref[...] * (-(B_HEAD_DIM ** -0.5) * LOG2E)
    lane = lax.broadcasted_iota(I32, (tq, LANES), 1)
    first = lane < B_HEAD_DIM
    qs = []
    for p in range(heads // 2):
        qp = qn[:, p * LANES:(p + 1) * LANES]
        qs += [jnp.where(first, qp, 0.0).astype(BF16), jnp.where(first, 0.0, qp).astype(BF16)]
    r = lax.broadcasted_iota(I32, (tq, tq), 0)
    c = lax.broadcasted_iota(I32, (tq, tq), 1)
    tri = jnp.where(r >= c, 1.0, 0.0).astype(BF16)
    causal = c < r

    def cols(h):
        return slice((h // 2) * LANES, (h // 2 + 1) * LANES)

    def stage_x(j, masked):
        start = pl.multiple_of(j * tq, tq)
        for h in range(heads):
            zn = _dot_nt(qs[h], k_ref[pl.ds(start, tq), cols(h)])
            l1m = jnp.minimum(zn, 0.0) - jnp.log2(1.0 + jnp.exp2(_neg_abs(zn)))
            if masked:
                l1m = jnp.where(causal, l1m, 0.0)
            cin = _dot(l1m.astype(BF16), tri)
            w = cin - zn
            if masked:
                w = jnp.where(causal, w, -jnp.inf)
            w_s[h] = w
            tot_s[h] = cin[:, 0:1]

    def stage_y(j):
        start = pl.multiple_of(j * tq, tq)
        for h in range(heads):
            car = car_s[h]
            a = jnp.exp2(w_s[h] + car)
            acc_s[h] += _dot(a.astype(BF16), v_ref[pl.ds(start, tq), cols(h)])
            car_s[h] = car + tot_s[h]

    acc_s[...] = jnp.zeros(acc_s.shape, F32)
    car_s[...] = jnp.zeros(car_s.shape, F32)
    stage_x(qi, True)

    def body(it, _):
        j = qi - 1 - it
        stage_y(j + 1)
        stage_x(j, False)
        return 0

    lax.fori_loop(0, qi, body, 0)
    stage_y(0)
    for p in range(heads // 2):
        o_ref[:, p * LANES:(p + 1) * LANES] = jnp.where(first, acc_s[2 * p], acc_s[2 * p + 1])


def sb_prompt(qsrc, kv, *, batch, seq, tq, heads):
    nq = seq // tq
    width = heads * B_HEAD_DIM
    groups = MAIN_WIDTH // width
    return pl.pallas_call(
        functools.partial(_sbp_kernel, tq=tq, heads=heads),
        grid=(batch, groups, nq),
        in_specs=[pl.BlockSpec((tq, width), lambda b, p, i: (b * nq + i, p)),
                  pl.BlockSpec((seq, width), lambda b, p, i: (b, p)),
                  pl.BlockSpec((seq, width), lambda b, p, i: (b, groups + p))],
        out_specs=pl.BlockSpec((tq, width), lambda b, p, i: (b * nq + i, p)),
        out_shape=jax.ShapeDtypeStruct((batch * seq, MAIN_WIDTH), F32),
        scratch_shapes=[pltpu.VMEM((heads, tq, tq), F32), pltpu.VMEM((heads, tq, 1), F32),
                        pltpu.VMEM((heads, tq, 1), F32), pltpu.VMEM((heads, tq, LANES), F32)],
        compiler_params=_cparams("parallel", "parallel", "arbitrary"),
        name="sb_prompt",
    )(qsrc, kv, kv)


def _sbs_kernel(qbd_ref, kn_ref, vn_ref, kc_ref, vc_ref, o_ref, acc_s, car_s, *, tsub, n_sub, tnew):
    j = pl.program_id(1)
    qbd = qbd_ref[0]
    width = qbd.shape[1]

    def block(k, v, tk, mask):
        r = lax.broadcasted_iota(I32, (tk, tk), 0)
        c = lax.broadcasted_iota(I32, (tk, tk), 1)
        tri = jnp.where(c >= r, 1.0, 0.0).astype(BF16)
        z = _dot(k.astype(BF16), qbd)
        l1m = -_softplus(z)
        if mask is not None:
            l1m = jnp.where(mask, l1m, 0.0)
        cin = _dot(tri, l1m.astype(BF16))
        a = jnp.exp(z + car_s[...] + cin)
        if mask is not None:
            a = jnp.where(mask, a, 0.0)
        car_s[...] += cin[0:1, :]
        acc_s[...] += _dot_tn(a.astype(BF16), v.astype(BF16))

    @pl.when(j == 0)
    def _():
        acc_s[...] = jnp.zeros(acc_s.shape, F32)
        car_s[...] = jnp.zeros(car_s.shape, F32)
        s_idx = lax.broadcasted_iota(I32, (tnew, width), 0)
        t_idx = lax.broadcasted_iota(I32, (tnew, width), 1) % tnew
        block(kn_ref[0], vn_ref[0], tnew, s_idx < t_idx)

    for sub in range(n_sub - 1, -1, -1):
        rows = slice(sub * tsub, (sub + 1) * tsub)
        block(kc_ref[0, rows, :], vc_ref[0, rows, :], tsub, None)

    @pl.when(j == pl.num_programs(1) - 1)
    def _():
        heads = width // tnew
        acc = acc_s[...].reshape(heads, tnew, acc_s.shape[1])
        hrow = lax.broadcasted_iota(I32, acc.shape, 0)
        hcol = lax.broadcasted_iota(I32, acc.shape, 2) // B_HEAD_DIM
        o_ref[0] = jnp.sum(jnp.where(hrow == hcol, acc, 0.0), axis=0)


def sb_sample(qbd, k_new, v_new, k_cache, v_cache, *, tblk, tsub):
    batch, past, width = k_cache.shape
    tnew = k_new.shape[1]
    nblk = past // tblk
    return pl.pallas_call(
        functools.partial(_sbs_kernel, tsub=tsub, n_sub=tblk // tsub, tnew=tnew),
        grid=(batch, nblk),
        in_specs=[pl.BlockSpec((1,) + qbd.shape[1:], lambda b, j: (b, 0, 0)),
                  pl.BlockSpec((1, tnew, width), lambda b, j: (b, 0, 0)),
                  pl.BlockSpec((1, tnew, width), lambda b, j: (b, 0, 0)),
                  pl.BlockSpec((1, tblk, width), lambda b, j: (b, nblk - 1 - j, 0)),
                  pl.BlockSpec((1, tblk, width), lambda b, j: (b, nblk - 1 - j, 0))],
        out_specs=pl.BlockSpec((1, tnew, width), lambda b, j: (b, 0, 0)),
        out_shape=jax.ShapeDtypeStruct((batch, tnew, width), F32),
        scratch_shapes=[pltpu.VMEM((qbd.shape[2], width), F32), pltpu.VMEM((1, qbd.shape[2]), F32)],
        compiler_params=_cparams("parallel", "arbitrary"),
        name="sb_sample",
    )(qbd, k_new, v_new, k_cache, v_cache)


def _layer_norm(x, g, b):
    mu = jnp.mean(x, axis=1, keepdims=True)
    xc = x - mu
    var = jnp.mean(xc * xc, axis=1, keepdims=True)
    return xc * lax.rsqrt(var + LN_EPS) * g + b


def _tail_kernel(*refs, n_main, head_major, prompt_tiles):
    x_ref = refs[0]
    mainp_refs = refs[1:1 + n_main]
    mains_refs = refs[1 + n_main:1 + 2 * n_main]
    rest = refs[1 + 2 * n_main:]
    hmp_ref, hms_ref, wo_ref, wom_ref, g_ref, b_ref, rwh_ref, rwl_ref, rb_ref = rest[:9]
    x1_ref, x1b_ref, meta_ref, metat_ref, stat_ref = rest[9:]
    t = x_ref.shape[0]
    is_sample = pl.program_id(0) >= prompt_tiles

    def pick(p_ref, s_ref):
        if head_major:
            return jnp.where(is_sample, s_ref[0], p_ref[0]).astype(BF16)
        return jnp.where(is_sample, s_ref[...], p_ref[...]).astype(BF16)

    a = _dot(jnp.where(is_sample, hms_ref[...], hmp_ref[...]).astype(BF16), wom_ref[...])
    if head_major:
        for h in range(n_main):
            a += _dot(pick(mainp_refs[h], mains_refs[h]), wo_ref[h])
    else:
        a += _dot(pick(mainp_refs[0], mains_refs[0]), wo_ref[...])
    x1 = _layer_norm(DN_ALPHA * x_ref[...] + a, g_ref[...], b_ref[...])
    x1_ref[...] = x1
    x1b_ref[...] = x1.astype(BF16)

    xh, xl = _split_bf16(x1)
    logits = _dot(xh, rwh_ref[...]) + _dot(xl, rwh_ref[...]) + _dot(xh, rwl_ref[...]) + rb_ref[...]
    lane = lax.broadcasted_iota(I32, (t, LANES), 1)
    logits = jnp.where(lane < N_EXPERTS, logits, -jnp.inf)
    lane_f = lane.astype(F32)
    vals, sels = [], []
    for _ in range(TOP_K):
        mx = jnp.max(logits, axis=1, keepdims=True)
        idx = jnp.min(jnp.where(logits == mx, lane_f, float(LANES)), axis=1, keepdims=True)
        sel = lane_f == idx
        vals.append(mx)
        sels.append(sel)
        logits = jnp.where(sel, -jnp.inf, logits)
    exps = [jnp.exp(v - vals[0]) for v in vals]
    tot = exps[0] + exps[1] + exps[2] + exps[3]
    wts = [e / tot for e in exps]

    onehot = jnp.zeros((t, LANES), F32)
    for sel in sels:
        onehot = jnp.where(sel, 1.0, onehot)
    cnt = jnp.sum(onehot, axis=0, keepdims=True)
    pc = jnp.floor((cnt + (PACK_ROWS - 1)) * (1.0 / PACK_ROWS))
    er = lax.broadcasted_iota(I32, (LANES, LANES), 0)
    ec = lax.broadcasted_iota(I32, (LANES, LANES), 1)
    before = jnp.where(er < ec, 1.0, 0.0).astype(BF16)
    off = _dot(jnp.broadcast_to(pc, (8, LANES)).astype(BF16), before)[0:1, :]
    tr = lax.broadcasted_iota(I32, (t, t), 0)
    tc = lax.broadcasted_iota(I32, (t, t), 1)
    earlier = jnp.where(tc < tr, 1.0, 0.0).astype(BF16)
    rank = _dot(earlier, onehot.astype(BF16))
    posfull = off * float(PACK_ROWS) + rank
    meta = jnp.zeros((t, LANES), F32)
    for k in range(TOP_K):
        pos_k = jnp.sum(jnp.where(sels[k], posfull, 0.0), axis=1, keepdims=True)
        meta = jnp.where(lane == k, pos_k, meta)
        meta = jnp.where(lane == TOP_K + k, wts[k], meta)
    meta_ref[...] = meta
    metat_ref[0] = meta.T[0:8, :]
    srow = lax.broadcasted_iota(I32, (8, LANES), 0)
    stat_ref[0] = jnp.where(srow == 0, pc, jnp.where(srow == 1, off, 0.0))


def mixer_tail(x, main_p, main_s, hm_p, hm_s, wo_main, wo_mem, ln_g, ln_b, rwh, rwl, rb, *, head_major):
    n_tok, d = x.shape
    t = TOK_TILE
    nt = n_tok // t
    assert hm_s.shape[0] == t and hm_p.shape[0] == (nt - 1) * t
    last_p = nt - 2
    if head_major:
        n_main, _, hd = main_p.shape
        mainp_specs = [pl.BlockSpec((1, t, hd), functools.partial(lambda h, i: (h, jnp.minimum(i, last_p), 0), h))
                       for h in range(n_main)]
        mains_specs = [pl.BlockSpec((1, t, hd), functools.partial(lambda h, i: (h, 0, 0), h))
                       for h in range(n_main)]
        wo_spec = pl.BlockSpec(wo_main.shape, lambda i: (0, 0, 0))
    else:
        n_main = 1
        mainp_specs = [pl.BlockSpec((t, MAIN_WIDTH), lambda i: (jnp.minimum(i, last_p), 0))]
        mains_specs = [pl.BlockSpec((t, MAIN_WIDTH), lambda i: (0, 0))]
        wo_spec = pl.BlockSpec(wo_main.shape, lambda i: (0, 0))
    const2 = lambda i: (0, 0)
    return pl.pallas_call(
        functools.partial(_tail_kernel, n_main=n_main, head_major=head_major, prompt_tiles=nt - 1),
        grid=(nt,),
        in_specs=[pl.BlockSpec((t, d), lambda i: (i, 0))] + mainp_specs + mains_specs + [
            pl.BlockSpec((t, MEM_WIDTH), lambda i: (jnp.minimum(i, last_p), 0)),
            pl.BlockSpec((t, MEM_WIDTH), const2),
            wo_spec,
            pl.BlockSpec(wo_mem.shape, const2),
            pl.BlockSpec((1, d), const2), pl.BlockSpec((1, d), const2),
            pl.BlockSpec((d, LANES), const2), pl.BlockSpec((d, LANES), const2),
            pl.BlockSpec((1, LANES), const2)],
        out_specs=[pl.BlockSpec((t, d), lambda i: (i, 0)),
                   pl.BlockSpec((t, d), lambda i: (i, 0)),
                   pl.BlockSpec((t, LANES), lambda i: (i, 0)),
                   pl.BlockSpec((1, 8, t), lambda i: (i, 0, 0)),
                   pl.BlockSpec((1, 8, LANES), lambda i: (i, 0, 0))],
        out_shape=[jax.ShapeDtypeStruct((n_tok, d), F32),
                   jax.ShapeDtypeStruct((n_tok, d), BF16),
                   jax.ShapeDtypeStruct((n_tok, LANES), F32),
                   jax.ShapeDtypeStruct((nt, 8, t), F32),
                   jax.ShapeDtypeStruct((nt, 8, LANES), F32)],
        compiler_params=_cparams("parallel"),
        name="mixer_tail",
    )(x, *([main_p] * n_main), *([main_s] * n_main), hm_p, hm_s, wo_main, wo_mem, ln_g, ln_b, rwh, rwl, rb)


def _tile_rows():
    need = TOK_TILE * TOP_K + N_EXPERTS * (PACK_ROWS - 1)
    return -(-need // 256) * 256


def _piece_copies(dst_ref, tile, make_copy, action, first=0, last=None):
    pieces = _tile_rows() // PACK_ROWS
    for p in range(first, pieces if last is None else last):
        g = pl.multiple_of(dst_ref[tile * pieces + p] * PACK_ROWS, PACK_ROWS)
        action(make_copy(p * PACK_ROWS, g), p)


def _start_alternating(cp, p):
    cp.start(priority=p % 2)


def _wait(cp, p):
    del p
    cp.wait()


def _zero_fill(ts_ref, tn_ref, nu_ref, xg_ref, zero_s, sem, action):
    def per_expert(e, _):
        def per_piece(p, _):
            g = pl.multiple_of((ts_ref[e] + p) * PACK_ROWS, PACK_ROWS)
            action(pltpu.make_async_copy(zero_s.at[pl.ds(0, PACK_ROWS)], xg_ref.at[pl.ds(g, PACK_ROWS)], sem))
            return 0

        lax.fori_loop(0, tn_ref[e], per_piece, 0)
        return 0

    lax.fori_loop(0, N_EXPERTS, per_expert, 0)

    def per_tile(i, _):
        g = pl.multiple_of(i * EXP_TILE, EXP_TILE)
        action(pltpu.make_async_copy(zero_s, xg_ref.at[pl.ds(g, EXP_TILE)], sem))
        return 0

    lax.fori_loop(nu_ref[0], xg_ref.shape[0] // EXP_TILE, per_tile, 0)


def _dispatch_kernel(dst_ref, ts_ref, tn_ref, nu_ref, metat_ref, x_ref, xg_ref,
                     xs_s, zero_s, sem, zsem):
    tile = pl.program_id(0)
    rt, t = xs_s.shape[1], x_ref.shape[0]
    buf = tile % 2

    @pl.when(tile == 0)
    def _():
        zero_s[...] = jnp.zeros(zero_s.shape, zero_s.dtype)
        _zero_fill(ts_ref, tn_ref, nu_ref, xg_ref, zero_s, zsem, lambda cp: cp.start())
        _zero_fill(ts_ref, tn_ref, nu_ref, xg_ref, zero_s, zsem, lambda cp: cp.wait())

    def copies_of(b):
        def make_copy(s, g):
            return pltpu.make_async_copy(xs_s.at[b, pl.ds(s, PACK_ROWS)], xg_ref.at[pl.ds(g, PACK_ROWS)],
                                         sem.at[b])
        return make_copy

    posr = metat_ref[0]
    x = x_ref[...]
    per_chunk = MOE_CHUNK // PACK_ROWS
    for c in range(rt // MOE_CHUNK):
        slot = (lax.broadcasted_iota(I32, (MOE_CHUNK, t), 0) + c * MOE_CHUNK).astype(F32)
        hit = slot == posr[0:1, :]
        for k in range(1, TOP_K):
            hit = hit | (slot == posr[k:k + 1, :])
        rows = slice(c * MOE_CHUNK, (c + 1) * MOE_CHUNK)
        xs_s[buf, rows, :] = _dot(jnp.where(hit, 1.0, 0.0).astype(BF16), x).astype(BF16)
        _piece_copies(dst_ref, tile, copies_of(buf), _start_alternating,
                      c * per_chunk, (c + 1) * per_chunk)

    @pl.when(tile > 0)
    def _():
        _piece_copies(dst_ref, tile - 1, copies_of(1 - buf), _wait)

    @pl.when(tile == pl.num_programs(0) - 1)
    def _():
        _piece_copies(dst_ref, tile, copies_of(buf), _wait)


def moe_dispatch(dstp, tail_start, tail_n, nused, metat, x1b, n_rows):
    n_tok, d = x1b.shape
    t = TOK_TILE
    rt = _tile_rows()
    return pl.pallas_call(
        _dispatch_kernel,
        grid_spec=pltpu.PrefetchScalarGridSpec(
            num_scalar_prefetch=4,
            grid=(n_tok // t,),
            in_specs=[pl.BlockSpec((1, 8, t), lambda i, *_: (i, 0, 0)),
                      pl.BlockSpec((t, d), lambda i, *_: (i, 0))],
            out_specs=pl.BlockSpec(memory_space=pl.ANY),
            scratch_shapes=[pltpu.VMEM((2, rt, d), BF16), pltpu.VMEM((EXP_TILE, d), BF16),
                            pltpu.SemaphoreType.DMA((2,)), pltpu.SemaphoreType.DMA(())]),
        out_shape=jax.ShapeDtypeStruct((n_rows + 2 * rt, d), BF16),
        compiler_params=_cparams("arbitrary"),
        name="moe_dispatch",
    )(dstp, tail_start, tail_n, nused, metat, x1b)


def _expert_kernel(te_ref, nu_ref, x_ref, wgu_ref, bgu_ref, wd_ref, bd_ref, y_ref, wgu_s, wd_s):
    i = pl.program_id(0)
    prev = te_ref[jnp.maximum(i - 1, 0)]

    @pl.when((i == 0) | (te_ref[i] != prev))
    def _():
        wgu_s[...] = wgu_ref[0].astype(BF16)
        wd_s[...] = wd_ref[0].astype(BF16)

    @pl.when(i < nu_ref[0])
    def _():
        x = x_ref[...]
        f = wd_s.shape[0]
        half = f // 2
        y = jnp.broadcast_to(bd_ref[0], y_ref.shape)
        for c in range(2):
            cols = slice(c * half, (c + 1) * half)
            ucols = slice(f + c * half, f + (c + 1) * half)
            gate = jnp.minimum(_dot(x, wgu_s[:, cols]) + bgu_ref[0, :, cols], SWIGLU_LIMIT)
            up = jnp.clip(_dot(x, wgu_s[:, ucols]) + bgu_ref[0, :, ucols], -SWIGLU_LIMIT, SWIGLU_LIMIT)
            act = (up + 1.0) * gate * jax.nn.sigmoid(SWIGLU_ALPHA * gate)
            y = y + _dot(act.astype(BF16), wd_s[cols, :])
        y_ref[...] = y.astype(y_ref.dtype)

    @pl.when(i >= nu_ref[0])
    def _():
        y_ref[...] = jnp.zeros(y_ref.shape, y_ref.dtype)


def moe_experts(te, nused, xg, wgu, bgu, wd, bd, n_rows):
    d = xg.shape[1]
    f2 = wgu.shape[-1]
    tm = EXP_TILE

    def row_map(i, te_ref, nu_ref):
        return (jnp.minimum(i, nu_ref[0] - 1), 0)

    def w_map(i, te_ref, nu_ref):
        return (te_ref[i], 0, 0)

    return pl.pallas_call(
        _expert_kernel,
        grid_spec=pltpu.PrefetchScalarGridSpec(
            num_scalar_prefetch=2,
            grid=(n_rows // tm,),
            in_specs=[pl.BlockSpec((tm, d), row_map),
                      pl.BlockSpec((1, d, f2), w_map),
                      pl.BlockSpec((1, 1, f2), w_map),
                      pl.BlockSpec((1, f2 // 2, d), w_map),
                      pl.BlockSpec((1, 1, d), w_map)],
            out_specs=pl.BlockSpec((tm, d), lambda i, *_: (i, 0)),
            scratch_shapes=[pltpu.VMEM((d, f2), BF16), pltpu.VMEM((f2 // 2, d), BF16)]),
        out_shape=jax.ShapeDtypeStruct((n_rows, d), BF16),
        compiler_params=_cparams("arbitrary"),
        name="moe_experts",
    )(te, nused, xg, wgu, bgu, wd, bd)


def _combine_kernel(dst_ref, meta_ref, x1_ref, g_ref, b_ref, yg_ref, x2_ref, ys_s, sem):
    tile = pl.program_id(0)
    last = pl.num_programs(0) - 1
    rt, t = ys_s.shape[1], x1_ref.shape[0]
    buf = tile % 2

    def copies_of(b):
        def make_copy(s, g):
            return pltpu.make_async_copy(yg_ref.at[pl.ds(g, PACK_ROWS)], ys_s.at[b, pl.ds(s, PACK_ROWS)],
                                         sem.at[b])
        return make_copy

    @pl.when(tile == 0)
    def _():
        _piece_copies(dst_ref, tile, copies_of(buf), _start_alternating)

    _piece_copies(dst_ref, tile, copies_of(buf), _wait)
    nxt = jnp.minimum(tile + 1, last)
    meta = meta_ref[...]
    f = jnp.zeros(x1_ref.shape, F32)
    per_chunk = MOE_CHUNK // PACK_ROWS
    for c in range(rt // MOE_CHUNK):
        _piece_copies(dst_ref, nxt, copies_of(1 - buf), _start_alternating,
                      c * per_chunk, (c + 1) * per_chunk)
        slot = (lax.broadcasted_iota(I32, (t, MOE_CHUNK), 1) + c * MOE_CHUNK).astype(F32)
        wmat = jnp.zeros((t, MOE_CHUNK), F32)
        for k in range(TOP_K):
            wmat = jnp.where(slot == meta[:, k:k + 1], meta[:, TOP_K + k:TOP_K + k + 1], wmat)
        f += _dot(wmat.astype(BF16), ys_s[buf, c * MOE_CHUNK:(c + 1) * MOE_CHUNK, :])
    x2_ref[...] = _layer_norm(DN_ALPHA * x1_ref[...] + f, g_ref[...], b_ref[...])

    @pl.when(tile == last)
    def _():
        _piece_copies(dst_ref, tile, copies_of(1 - buf), _wait)


def moe_combine(dstp, meta, x1, ln_g, ln_b, yg):
    n_tok, d = x1.shape
    t = TOK_TILE
    rt = _tile_rows()
    return pl.pallas_call(
        _combine_kernel,
        grid_spec=pltpu.PrefetchScalarGridSpec(
            num_scalar_prefetch=1,
            grid=(n_tok // t,),
            in_specs=[pl.BlockSpec((t, LANES), lambda i, *_: (i, 0)),
                      pl.BlockSpec((t, d), lambda i, *_: (i, 0)),
                      pl.BlockSpec((1, d), lambda i, *_: (0, 0)),
                      pl.BlockSpec((1, d), lambda i, *_: (0, 0)),
                      pl.BlockSpec(memory_space=pl.ANY)],
            out_specs=pl.BlockSpec((t, d), lambda i, *_: (i, 0)),
            scratch_shapes=[pltpu.VMEM((2, rt, d), BF16), pltpu.SemaphoreType.DMA((2,))]),
        out_shape=jax.ShapeDtypeStruct((n_tok, d), F32),
        compiler_params=_cparams("arbitrary"),
        name="moe_combine",
    )(dstp, meta, x1, ln_g, ln_b, yg)


def _route_tables(stats, n_rows):
    pc = stats[:, 0, :N_EXPERTS].astype(I32)
    off = stats[:, 1, :N_EXPERTS].astype(I32)
    pieces_e = jnp.sum(pc, axis=0)
    per_tile = EXP_TILE // PACK_ROWS
    tiles_e = (pieces_e + per_tile - 1) // per_tile
    ends = jnp.cumsum(tiles_e)
    base_e = (ends - tiles_e) * per_tile
    gd = base_e[None, :] + jnp.cumsum(pc, axis=0) - pc
    tail_start = base_e + pieces_e
    tail_n = tiles_e * per_tile - pieces_e
    nused = ends[-1]
    tile_ids = jnp.minimum(jnp.arange(n_rows // EXP_TILE, dtype=I32), nused - 1)
    te = jnp.sum((ends[None, :] <= tile_ids[:, None]).astype(I32), axis=1)
    seg_end = off + pc
    pidx = jnp.arange(_tile_rows() // PACK_ROWS, dtype=I32)
    e_of_p = jnp.sum((seg_end[:, None, :] <= pidx[None, :, None]).astype(I32), axis=2)
    sel = e_of_p[:, :, None] == jnp.arange(N_EXPERTS, dtype=I32)[None, None, :]
    dstp = jnp.sum(jnp.where(sel, (gd - off)[:, None, :], 0), axis=2) + pidx[None, :]
    used = pidx[None, :] < seg_end[:, -1:]
    spare = (n_rows // PACK_ROWS + (jnp.arange(pc.shape[0], dtype=I32) % 2)[:, None] * pidx.shape[0]
             + pidx[None, :])
    dst_disp = jnp.where(used, dstp, spare)
    dst_comb = jnp.where(used, dstp, 0)
    return (dst_disp.reshape(-1).astype(I32), dst_comb.reshape(-1).astype(I32), tail_start.astype(I32),
            tail_n.astype(I32), te, nused.reshape(1).astype(I32))


def moe_layer(x1, x1b, meta, metat, stats, ln_g, ln_b, wgu, bgu, wd, bd, layer):
    n_tok = x1.shape[0]
    nt = n_tok // TOK_TILE
    worst = n_tok * TOP_K + nt * N_EXPERTS * (PACK_ROWS - 1) + N_EXPERTS * (EXP_TILE - 1)
    n_rows = -(-worst // EXP_TILE) * EXP_TILE
    dst_disp, dst_comb, tail_start, tail_n, te, nused = _route_tables(stats, n_rows)
    xg = moe_dispatch(dst_disp, tail_start, tail_n, nused, metat, x1b, n_rows)
    yg = moe_experts(te + layer * N_EXPERTS, nused, xg, wgu, bgu, wd, bd, n_rows)
    return moe_combine(dst_comb, meta, x1, ln_g, ln_b, yg)


def _pad_cols(w, width):
    return jnp.pad(w, ((0, 0), (0, width - w.shape[1])))


def _hi_lo(w):
    hi = w.astype(BF16)
    return hi, (w - hi.astype(F32)).astype(BF16)


def kernel(x_prompt, x_sample, state_mlstm_C, state_mlstm_n, state_mlstm_m, cache_sb_k, cache_sb_v,
           cache_mem_k, cache_mem_v, mem_prompt, w_in_a, b_gate_a, w_in_b, w_kv_b, w_mem_kv, w_out,
           ln_g, ln_b, router_w, router_b, w_gate_up, b_gate_up, w_down, b_down):
    bp, seq, d = x_prompt.shape
    bs, dseq, _ = x_sample.shape
    n_mem = mem_prompt.shape[1]
    past = cache_sb_k.shape[1]
    n_p = bp * seq
    n_s = bs * dseq
    hd = A_HEAD_DIM
    x = jnp.concatenate([x_prompt.reshape(n_p, d), x_sample.reshape(n_s, d)], axis=0)
    n_tok = n_p + n_s
    tm = n_tok // 26 if n_tok % 26 == 0 and (n_tok // 26) % 8 == 0 else TOK_TILE

    wm = jnp.transpose(w_mem_kv, (1, 0, 2)).reshape(d, DEPTH * 2 * MEM_WIDTH).astype(BF16)
    (mkv,) = matmul(mem_prompt.reshape(bp * n_mem, d), wm, tm=512, tn=512)
    mkv = mkv.reshape(bp, n_mem, DEPTH, 2, MEM_WIDTH)
    p_mem_k = jnp.transpose(mkv[:, :, :, 0], (2, 0, 1, 3))
    p_mem_v = jnp.transpose(mkv[:, :, :, 1], (2, 0, 1, 3))
    s_mem_k = cache_mem_k.reshape(DEPTH, bs, n_mem, MEM_WIDTH)
    s_mem_v = cache_mem_v.reshape(DEPTH, bs, n_mem, MEM_WIDTH)

    def pack_nm(n, m):
        b = n.shape[0]
        out = jnp.zeros((b, A_HEADS, 8, 2 * LANES), F32)
        out = out.at[:, :, 0, :hd].set(n)
        return out.at[:, :, 1, :].set(jnp.broadcast_to(m[:, :, None], (b, A_HEADS, 2 * LANES)))

    k_cache = cache_sb_k.reshape(bs, past, MAIN_WIDTH)
    v_cache = cache_sb_v.reshape(bs, past, MAIN_WIDTH)
    n_exp = DEPTH * N_EXPERTS
    wgu_all = w_gate_up.reshape(n_exp, d, 2 * D_FF)
    bgu_all = b_gate_up.reshape(n_exp, 1, 2 * D_FF)
    wd_all = w_down.reshape(n_exp, D_FF, d)
    bd_all = b_down.reshape(n_exp, 1, d)

    cs, ns, ms = [], [], []
    k_new = v_new = kv16 = None
    for l in range(DEPTH):
        rwh, rwl = _hi_lo(_pad_cols(router_w[l], LANES))
        rb = _pad_cols(router_b[l][None, :], LANES)
        wo = w_out[l].astype(BF16)
        if l < N_A:
            w = w_in_a[l]
            wg = jnp.transpose(w[:, :4 * MAIN_WIDTH].reshape(d, 4 * A_HEADS, hd), (1, 0, 2)).astype(BF16)
            wgh, wgl = _hi_lo(_pad_cols(w[:, 4 * MAIN_WIDTH:4 * MAIN_WIDTH + 2 * A_HEADS], LANES))
            wqm = w[:, 4 * MAIN_WIDTH + 2 * A_HEADS:].astype(BF16)
            p, qm, gates = proj_a(x, wg, wqm, jnp.concatenate([wgh, wgl], axis=1), tm=tm)
            gates = gates[:, :2 * A_HEADS] + b_gate_a[l][None, :]
            g_p = jnp.transpose(gates[:n_p].reshape(bp, seq, 2 * A_HEADS), (0, 2, 1))
            g_s = jnp.transpose(gates[n_p:].reshape(bs, dseq, 2 * A_HEADS), (0, 2, 1))
            zc = jnp.zeros((bp, A_HEADS, hd, hd), F32)
            znm = jnp.zeros((bp, A_HEADS, 8, 2 * LANES), F32)
            h_p, c_p, nm_p = mlstm(p, g_p, zc, znm, batch=bp, seq=seq, row_base=0,
                                   rows_per_step=4 * MLSTM_CHUNK, chunk=MLSTM_CHUNK)
            h_s, c_s, nm_s = mlstm(p, g_s, state_mlstm_C[l], pack_nm(state_mlstm_n[l], state_mlstm_m[l]),
                                   batch=bs, seq=dseq, row_base=n_p, rows_per_step=dseq, chunk=dseq)
            cs.append((c_p, c_s))
            ns.append((nm_p[:, :, 0, :hd], nm_s[:, :, 0, :hd]))
            ms.append((nm_p[:, :, 1, 0], nm_s[:, :, 1, 0]))
            hm_p = mem_attend(qm, 0, p_mem_k[l], p_mem_v[l], batch=bp, seq=seq, row_base=0, tq=512)
            hm_s = mem_attend(qm, 0, s_mem_k[l], s_mem_v[l], batch=bs, seq=dseq, row_base=n_p, tq=dseq)
            wo_main = wo[:MAIN_WIDTH].reshape(A_HEADS, hd, d)
            head_major = True
        else:
            if l == N_A:
                kv32, kv16 = matmul(x, w_kv_b.astype(BF16), tm=tm, tn=512, out_dtypes=(F32, BF16))
                k_new = kv32[:, :MAIN_WIDTH]
                v_new = kv32[:, MAIN_WIDTH:]
            (pq,) = matmul(x, w_in_b[l - N_A].astype(BF16), tm=tm, tn=512)
            h_p = sb_prompt(pq, kv16, batch=bp, seq=seq, tq=256, heads=12)
            q_s = pq[n_p:, :MAIN_WIDTH].reshape(bs, dseq, B_HEADS, B_HEAD_DIM) * (B_HEAD_DIM ** -0.5)
            qbd = jnp.einsum("bthd,hg->bhdgt", q_s, jnp.eye(B_HEADS, dtype=F32))
            qbd = qbd.reshape(bs, MAIN_WIDTH, B_HEADS * dseq).astype(BF16)
            h_s = sb_sample(qbd, k_new[n_p:].reshape(bs, dseq, MAIN_WIDTH),
                            v_new[n_p:].reshape(bs, dseq, MAIN_WIDTH), k_cache, v_cache,
                            tblk=1024, tsub=256).reshape(n_s, MAIN_WIDTH)
            hm_p = mem_attend(pq, MAIN_WIDTH // MEM_WIDTH, p_mem_k[l], p_mem_v[l],
                              batch=bp, seq=seq, row_base=0, tq=512)
            hm_s = mem_attend(pq, MAIN_WIDTH // MEM_WIDTH, s_mem_k[l], s_mem_v[l],
                              batch=bs, seq=dseq, row_base=n_p, tq=dseq)
            wo_main = wo[:MAIN_WIDTH]
            head_major = False
        x1, x1b, meta, metat, stats = mixer_tail(
            x, h_p, h_s, hm_p, hm_s, wo_main, wo[MAIN_WIDTH:], ln_g[l, 0][None, :], ln_b[l, 0][None, :],
            rwh, rwl, rb, head_major=head_major)
        x = moe_layer(x1, x1b, meta, metat, stats, ln_g[l, 1][None, :], ln_b[l, 1][None, :],
                      wgu_all, bgu_all, wd_all, bd_all, l)

    y_prompt = x[:n_p].reshape(bp, seq, d)
    y_sample = x[n_p:].reshape(bs, dseq, d)
    p_c = jnp.stack([c[0] for c in cs])
    s_c = jnp.stack([c[1] for c in cs])
    p_n = jnp.stack([n[0] for n in ns])
    s_n = jnp.stack([n[1] for n in ns])
    p_m = jnp.stack([m[0] for m in ms])
    s_m = jnp.stack([m[1] for m in ms])
    p_sb_k = k_new[:n_p].reshape(bp, seq, B_HEADS, B_HEAD_DIM)
    p_sb_v = v_new[:n_p].reshape(bp, seq, B_HEADS, B_HEAD_DIM)
    s_sb_k = k_new[n_p:].reshape(bs, dseq, B_HEADS, B_HEAD_DIM)
    s_sb_v = v_new[n_p:].reshape(bs, dseq, B_HEADS, B_HEAD_DIM)
    pmk = p_mem_k.reshape(DEPTH, bp, n_mem, MEM_HEADS, MEM_HEAD_DIM)
    pmv = p_mem_v.reshape(DEPTH, bp, n_mem, MEM_HEADS, MEM_HEAD_DIM)
    return (y_prompt, y_sample, p_c, p_n, p_m, p_sb_k, p_sb_v, pmk, pmv, s_c, s_n, s_m, s_sb_k, s_sb_v)
```
